```python
import math
import jax, jax.numpy as jnp
from jax import lax
import numpy as np

D_MODEL = 2048
BATCH = 2
SEQ = 16384
DEPTH = 2

CHUNK = 64
D_HGRN = D_MODEL // 2
HGRN_HEAD_DIM = 128
HGRN_HEADS = D_HGRN // HGRN_HEAD_DIM
D_S5 = D_MODEL // 2
S5_GROUP = 16
S5_GROUPS = D_S5 // S5_GROUP
S5_STATE = 64
S5_MIN_NEG = 1e-4
D_FF_DENSE = 5632
N_EXPERTS = 8
TOP_K = 2
D_FF_EXPERT = 7168
MOE_BLOCK = 1024
N_DENSE = (DEPTH + 1) // 2
N_MOE = DEPTH // 2
RMS_EPS = 1e-6
D_IN = 4 * D_HGRN + D_S5 + 2 * D_MODEL
SPLIT_POINTS = (D_HGRN, 2 * D_HGRN, 3 * D_HGRN, 4 * D_HGRN,
                4 * D_HGRN + D_S5, 4 * D_HGRN + D_S5 + D_MODEL)

kernel_name = "hybrid_hgrn2_s5_moe_trunk"


def rmsnorm(x, w):
    xf = x.astype(jnp.float32)
    y = xf * lax.rsqrt(jnp.mean(xf * xf, axis=-1, keepdims=True) + RMS_EPS)
    return (y * w.astype(jnp.float32)).astype(x.dtype)


def hgrn2_mixer(q_pre, f_pre, i_in, g_pre, lb, norm_w):
    f32 = jnp.float32
    bsz, seq, _ = q_pre.shape
    n_chunks = seq // CHUNK
    q = jax.nn.silu(q_pre.astype(f32))
    lbf = lb.astype(f32)
    log_f = jnp.logaddexp(jnp.log(lbf), jnp.log1p(-lbf) + jax.nn.log_sigmoid(f_pre.astype(f32)))
    k = -jnp.expm1(log_f)
    v = i_in.astype(f32)

    def to_chunks(t):
        return t.reshape(bsz, n_chunks, CHUNK, HGRN_HEADS, HGRN_HEAD_DIM).transpose(1, 0, 3, 2, 4)

    causal = jnp.tril(jnp.ones((CHUNK, CHUNK), dtype=bool))

    def step(state, inp):
        qc, kc, vc, gc = inp
        G = jnp.cumsum(gc, axis=2)
        diff = G[:, :, :, None, :] - G[:, :, None, :, :]
        decay = jnp.where(causal[:, :, None], jnp.exp(jnp.minimum(diff, 0.0)), 0.0)
        scores = jnp.einsum('bhtk,bhsk,bhtsk->bhts', qc, kc, decay)
        o = (jnp.einsum('bhts,bhsv->bhtv', scores, vc)
             + jnp.einsum('bhtk,bhkv->bhtv', qc * jnp.exp(G), state))
        G_last = G[:, :, -1:, :]
        new_state = (jnp.exp(G_last[:, :, 0, :])[..., None] * state
                     + jnp.einsum('bhsk,bhsv->bhkv', kc * jnp.exp(G_last - G), vc))
        return new_state, o

    s0 = jnp.zeros((bsz, HGRN_HEADS, HGRN_HEAD_DIM, HGRN_HEAD_DIM), f32)
    _, o = lax.scan(step, s0, (to_chunks(q), to_chunks(k), to_chunks(v), to_chunks(log_f)))
    o = o.transpose(1, 0, 3, 2, 4).reshape(bsz, seq, HGRN_HEADS, HGRN_HEAD_DIM)
    o = o * lax.rsqrt(jnp.mean(o * o, axis=-1, keepdims=True) + RMS_EPS) * norm_w.astype(f32)
    o = o.reshape(bsz, seq, D_HGRN) * jax.nn.silu(g_pre.astype(f32))
    return o.astype(q_pre.dtype)


def s5_mixer(u, a_re, a_im, log_dt, b_re, b_im, c_re, c_im, d_skip, w_glu):
    f32 = jnp.float32
    bsz, seq, _ = u.shape
    n_chunks = seq // CHUNK
    uf = u.astype(f32)
    A = lax.complex(jnp.minimum(a_re.astype(f32), -S5_MIN_NEG), a_im.astype(f32))
    dt = jnp.exp(log_dt.astype(f32))[:, None]
    A_bar = jnp.exp(A * dt)
    B = lax.complex(b_re.astype(f32), b_im.astype(f32))
    B_bar = ((A_bar - 1.0) / A)[..., None] * B
    C = lax.complex(c_re.astype(f32), c_im.astype(f32))
    u_chunks = uf.reshape(bsz, n_chunks, CHUNK, S5_GROUPS, S5_GROUP).transpose(1, 0, 2, 3, 4)
    a_elems = jnp.broadcast_to(A_bar, (bsz, CHUNK, S5_GROUPS, S5_STATE))

    def combine(left, right):
        a_l, b_l = left
        a_r, b_r = right
        return a_l * a_r, a_r * b_l + b_r

    def step(h, uc):
        bu = jnp.einsum('bcgi,gpi->bcgp', uc.astype(B_bar.dtype), B_bar)
        a_cum, h_loc = lax.associative_scan(combine, (a_elems, bu), axis=1)
        hs = a_cum * h[:, None] + h_loc
        y = jnp.einsum('bcgp,gip->bcgi', hs, C).real
        return hs[:, -1], y

    h0 = jnp.zeros((bsz, S5_GROUPS, S5_STATE), B_bar.dtype)
    _, y = lax.scan(step, h0, u_chunks)
    y = y.transpose(1, 0, 2, 3, 4).reshape(bsz, seq, D_S5) + d_skip.astype(f32) * uf
    z = jax.nn.gelu(y)
    z = z * jax.nn.sigmoid(z @ w_glu.astype(f32))
    return z.astype(u.dtype)


def swiglu(h, w_gate, w_up, w_down):
    return (jax.nn.silu(h @ w_gate) * (h @ w_up)) @ w_down


def moe_swiglu(h, w_router, w_gate, w_up, w_down):
    bsz, seq, dm = h.shape
    xt = h.reshape(-1, dm)
    T = xt.shape[0]
    n_assign = T * TOP_K
    logits = (xt @ w_router).astype(jnp.float32)
    top_logits, top_idx = lax.top_k(logits, TOP_K)
    gates = jax.nn.softmax(top_logits, axis=-1)
    e_flat = top_idx.reshape(-1).astype(jnp.int32)
    g_flat = gates.reshape(-1)
    tok_flat = jnp.arange(n_assign, dtype=jnp.int32) // TOP_K
    order = jnp.argsort(e_flat)
    e_sorted = e_flat[order]
    counts = jnp.bincount(e_flat, length=N_EXPERTS).astype(jnp.int32)
    starts = jnp.cumsum(counts) - counts
    padded = (counts + MOE_BLOCK - 1) // MOE_BLOCK * MOE_BLOCK
    pad_ends = jnp.cumsum(padded)
    pad_starts = pad_ends - padded
    dest = pad_starts[e_sorted] + (jnp.arange(n_assign, dtype=jnp.int32) - starts[e_sorted])
    n_rows = (n_assign + MOE_BLOCK - 1) // MOE_BLOCK * MOE_BLOCK + N_EXPERTS * MOE_BLOCK
    n_blocks = n_rows // MOE_BLOCK
    row_tok = jnp.full((n_rows,), T, jnp.int32).at[dest].set(tok_flat[order])
    row_gate = jnp.zeros((n_rows,), jnp.float32).at[dest].set(g_flat[order])
    blk_start = jnp.arange(n_blocks, dtype=jnp.int32) * MOE_BLOCK
    blk_exp = jnp.minimum(jnp.searchsorted(pad_ends, blk_start, side='right'), N_EXPERTS - 1)
    x_pad = jnp.concatenate([xt, jnp.zeros((1, dm), xt.dtype)], axis=0)

    def expert_block(args):
        tok, e = args
        xb = x_pad[tok]
        return (jax.nn.silu(xb @ w_gate[e]) * (xb @ w_up[e])) @ w_down[e]

    y_rows = lax.map(expert_block, (row_tok.reshape(n_blocks, MOE_BLOCK), blk_exp))
    y_rows = (y_rows.reshape(n_rows, dm).astype(jnp.float32) * row_gate[:, None]).astype(h.dtype)
    out = jnp.zeros((T + 1, dm), h.dtype).at[row_tok].add(y_rows)[:T]
    return out.reshape(bsz, seq, dm)


def setup_inputs(seed: int = 0) -> dict:
    key = jax.random.key(seed)
    ks = jax.random.split(key, 32)

    def nrm(k, shape, scale):
        return jax.random.normal(k, shape, jnp.float32) * scale

    n_idx = jnp.arange(S5_STATE, dtype=jnp.float32)
    return {
        "x": nrm(ks[0], (BATCH, SEQ, D_MODEL), 1.0),
        "attn_norm_w": 1.0 + nrm(ks[1], (DEPTH, D_MODEL), 0.02),
        "w_in": nrm(ks[2], (DEPTH, D_MODEL, D_IN), D_MODEL ** -0.5),
        "hgrn_lb_logits": nrm(ks[3], (DEPTH, D_HGRN), 0.5),
        "hgrn_norm_w": 1.0 + nrm(ks[4], (DEPTH, HGRN_HEAD_DIM), 0.02),
        "s5_a_re": -0.5 + nrm(ks[5], (DEPTH, S5_GROUPS, S5_STATE), 0.01),
        "s5_a_im": math.pi * n_idx + nrm(ks[6], (DEPTH, S5_GROUPS, S5_STATE), 0.01),
        "s5_log_dt": jax.random.uniform(ks[7], (DEPTH, S5_GROUPS), jnp.float32,
                                        math.log(1e-3), math.log(1e-1)),
        "s5_b_re": nrm(ks[8], (DEPTH, S5_GROUPS, S5_STATE, S5_GROUP), (2 * S5_GROUP) ** -0.5),
        "s5_b_im": nrm(ks[9], (DEPTH, S5_GROUPS, S5_STATE, S5_GROUP), (2 * S5_GROUP) ** -0.5),
        "s5_c_re": nrm(ks[10], (DEPTH, S5_GROUPS, S5_GROUP, S5_STATE), S5_STATE ** -0.5),
        "s5_c_im": nrm(ks[11], (DEPTH, S5_GROUPS, S5_GROUP, S5_STATE), S5_STATE ** -0.5),
        "s5_d": nrm(ks[12], (DEPTH, D_S5), 1.0),
        "s5_w_glu": nrm(ks[13], (DEPTH, D_S5, D_S5), D_S5 ** -0.5),
        "w_branch_a": nrm(ks[14], (DEPTH, D_HGRN, D_MODEL), D_HGRN ** -0.5),
        "w_branch_b": nrm(ks[15], (DEPTH, D_S5, D_MODEL), D_S5 ** -0.5),
        "w_out": nrm(ks[16], (DEPTH, D_MODEL, D_MODEL), D_MODEL ** -0.5),
        "ffn_norm_w": 1.0 + nrm(ks[17], (DEPTH, D_MODEL), 0.02),
        "dense_w_gate": nrm(ks[18], (N_DENSE, D_MODEL, D_FF_DENSE), D_MODEL ** -0.5),
        "dense_w_up": nrm(ks[19], (N_DENSE, D_MODEL, D_FF_DENSE), D_MODEL ** -0.5),
        "dense_w_down": nrm(ks[20], (N_DENSE, D_FF_DENSE, D_MODEL), D_FF_DENSE ** -0.5),
        "moe_w_router": nrm(ks[21], (N_MOE, D_MODEL, N_EXPERTS), D_MODEL ** -0.5),
        "moe_w_gate": nrm(ks[22], (N_MOE, N_EXPERTS, D_MODEL, D_FF_EXPERT), D_MODEL ** -0.5),
        "moe_w_up": nrm(ks[23], (N_MOE, N_EXPERTS, D_MODEL, D_FF_EXPERT), D_MODEL ** -0.5),
        "moe_w_down": nrm(ks[24], (N_MOE, N_EXPERTS, D_FF_EXPERT, D_MODEL), D_FF_EXPERT ** -0.5),
        "final_norm_w": 1.0 + nrm(ks[25], (D_MODEL,), 0.02),
    }


def reference(x, attn_norm_w, w_in, hgrn_lb_logits, hgrn_norm_w, s5_a_re, s5_a_im, s5_log_dt,
              s5_b_re, s5_b_im, s5_c_re, s5_c_im, s5_d, s5_w_glu, w_branch_a, w_branch_b,
              w_out, ffn_norm_w, dense_w_gate, dense_w_up, dense_w_down, moe_w_router,
              moe_w_gate, moe_w_up, moe_w_down, final_norm_w):
    lower_bounds = jnp.cumsum(jax.nn.softmax(hgrn_lb_logits.astype(jnp.float32), axis=0), axis=0)
    lower_bounds = lower_bounds - lower_bounds[0]
    for layer in range(DEPTH):
        h = rmsnorm(x, attn_norm_w[layer])
        proj = h @ w_in[layer]
        q_a, f_a, i_a, g_a, u_b, gate_a, gate_b = jnp.split(proj, SPLIT_POINTS, axis=-1)
        y_a = hgrn2_mixer(q_a, f_a, i_a, g_a, lower_bounds[layer], hgrn_norm_w[layer])
        y_b = s5_mixer(u_b, s5_a_re[layer], s5_a_im[layer], s5_log_dt[layer], s5_b_re[layer],
                       s5_b_im[layer], s5_c_re[layer], s5_c_im[layer], s5_d[layer], s5_w_glu[layer])
        merged = (jax.nn.sigmoid(gate_a) * (y_a @ w_branch_a[layer])
                  + jax.nn.sigmoid(gate_b) * (y_b @ w_branch_b[layer]))
        x = x + merged @ w_out[layer]
        h = rmsnorm(x, ffn_norm_w[layer])
        if layer % 2 == 0:
            j = layer // 2
            x = x + swiglu(h, dense_w_gate[j], dense_w_up[j], dense_w_down[j])
        else:
            j = layer // 2
            x = x + moe_swiglu(h, moe_w_router[j], moe_w_gate[j], moe_w_up[j], moe_w_down[j])
    return rmsnorm(x, final_norm_w)
```

```python
import functools
import math

import numpy as np
import jax
import jax.numpy as jnp
from jax import lax
from jax.experimental import pallas as pl
from jax.experimental.pallas import tpu as pltpu

F32 = jnp.float32
BF16 = jnp.bfloat16

RMS_EPS = 1e-6
HGRN_HEAD_DIM = 128
S5_GROUP = 16
S5_STATE = 64
S5_MIN_NEG = 1e-4
TOP_K = 2

LANE = 128
VMEM_LIMIT = 56 * 1024 * 1024

HGRN_CHUNK = 128
S5_L = 16
S5_ROWS = 512
MOE_BLOCK = 1024
GROUPS_PER_TILE = LANE // S5_GROUP
STATE_PER_TILE = GROUPS_PER_TILE * S5_STATE


def _cparams(sem):
    return pltpu.CompilerParams(dimension_semantics=sem, vmem_limit_bytes=VMEM_LIMIT)


def _dot(a, b):
    return jnp.dot(a, b, preferred_element_type=F32)


def _dot_nt(a, b):
    return lax.dot_general(a, b, (((1,), (1,)), ((), ())), preferred_element_type=F32)


def _dot_tn(a, b):
    return lax.dot_general(a, b, (((0,), (0,)), ((), ())), preferred_element_type=F32)


def _sigmoid(x):
    return 1.0 / (1.0 + jnp.exp(-x))


def _rmsnorm(x, w):
    ms = jnp.mean(x * x, axis=-1, keepdims=True)
    return x * lax.rsqrt(ms + RMS_EPS) * w


def _norm_inproj_kernel(x_ref, nw_ref, w_ref, qig_ref, gates_ref, f_ref, u_ref, h_scr):
    j = pl.program_id(1)

    @pl.when(j == 0)
    def _():
        h_scr[...] = _rmsnorm(x_ref[...], nw_ref[...]).astype(BF16)

    acc = _dot(h_scr[...], w_ref[...])

    @pl.when(j < 3)
    def _():
        qig_ref[...] = acc.astype(BF16)

    @pl.when((j >= 3) & (j < 7))
    def _():
        gates_ref[...] = acc.astype(BF16)

    @pl.when(j == 7)
    def _():
        f_ref[...] = acc

    @pl.when(j == 8)
    def _():
        for q in range(u_ref.shape[0]):
            u_ref[q] = acc[:, q * LANE:(q + 1) * LANE].astype(BF16)


def _norm_inproj(x, norm_w, w_r, dh):
    T, D = x.shape
    tm = min(512, T)
    nq = dh // LANE
    grid = (T // tm, 9)
    return pl.pallas_call(
        _norm_inproj_kernel,
        grid=grid,
        in_specs=[
            pl.BlockSpec((tm, D), lambda i, j: (i, 0)),
            pl.BlockSpec((1, D), lambda i, j: (0, 0)),
            pl.BlockSpec((D, dh), lambda i, j: (0, j)),
        ],
        out_specs=[
            pl.BlockSpec((tm, dh), lambda i, j: (i, jnp.minimum(j, 2))),
            pl.BlockSpec((tm, dh), lambda i, j: (i, jnp.clip(j - 3, 0, 3))),
            pl.BlockSpec((tm, dh), lambda i, j: (i, 0)),
            pl.BlockSpec((nq, tm, LANE), lambda i, j: (0, i, 0)),
        ],
        out_shape=[
            jax.ShapeDtypeStruct((T, 3 * dh), BF16),
            jax.ShapeDtypeStruct((T, 4 * dh), BF16),
            jax.ShapeDtypeStruct((T, dh), F32),
            jax.ShapeDtypeStruct((nq, T, LANE), BF16),
        ],
        scratch_shapes=[pltpu.VMEM((tm, D), BF16)],
        compiler_params=_cparams(("arbitrary", "arbitrary")),
        name="norm_inproj",
    )(x, norm_w.reshape(1, D), w_r)


def _hgrn_tables(C):
    nlev = int(math.log2(C))
    r = np.arange(C)[:, None]
    c = np.arange(C)[None, :]
    mats = [(c <= r), (c > r)]
    masks = [np.eye(C, dtype=bool)]
    for lev in range(1, nlev + 1):
        b = 2 ** lev
        half = b // 2
        mid = (r // b) * b + half
        upper = r >= mid
        mats.append(np.where(upper, (c >= mid) & (c <= r), (c > r) & (c < mid)))
        masks.append(((r // b) == (c // b)) & ((r % b) >= half) & ((c % b) < half))
    w = np.concatenate(mats, axis=0).astype(np.float32)
    wcat = np.concatenate([w, w], axis=1)
    m = np.stack(masks, axis=0).astype(np.float32)
    return wcat, m, nlev


def _hgrn2_kernel(q_ref, i_ref, g_ref, f_ref, lb_ref, nw_ref, wcat_ref, mask_ref,
                  o_ref, st_ref, *, C, nlev, heads):
    hd = HGRN_HEAD_DIM

    @pl.when(pl.program_id(1) == 0)
    def _():
        st_ref[...] = jnp.zeros_like(st_ref)

    x = f_ref[...]
    e = jnp.exp(-jnp.abs(x))
    log_sig = jnp.minimum(x, 0.0) - jnp.log(1.0 + e)
    a_ = lb_ref[0:1, :]
    b_ = lb_ref[1:2, :] + log_sig
    logf = jnp.maximum(a_, b_) + jnp.log(1.0 + jnp.exp(-jnp.abs(a_ - b_)))
    kk = lb_ref[2:3, :] * jnp.where(x >= 0.0, e, 1.0) / (1.0 + e)
    q = q_ref[...].astype(F32)
    qs = q * _sigmoid(q)
    v = i_ref[...]

    hi = logf.astype(BF16)
    lo = (logf - hi.astype(F32)).astype(BF16)
    hl = jnp.concatenate([hi, lo], axis=0)

    def window(idx):
        return jnp.exp(_dot(wcat_ref[idx * C:(idx + 1) * C, :], hl))

    e_q = window(0)
    q_in = (qs * e_q).astype(BF16)
    dec = e_q[C - 1:C, :]
    k_st = (kk * window(1)).astype(BF16)

    scores = [None] * heads
    for lev in range(nlev + 1):
        if lev == 0:
            ql = qs.astype(BF16)
            kl = kk.astype(BF16)
        else:
            e_l = window(1 + lev)
            ql = (qs * e_l).astype(BF16)
            kl = (kk * e_l).astype(BF16)
        m = mask_ref[lev]
        for h in range(heads):
            sl = slice(h * hd, (h + 1) * hd)
            p = _dot_nt(ql[:, sl], kl[:, sl]) * m
            scores[h] = p if scores[h] is None else scores[h] + p

    outs = []
    for h in range(heads):
        sl = slice(h * hd, (h + 1) * hd)
        st = st_ref[h]
        o_h = _dot(scores[h].astype(BF16), v[:, sl]) + _dot_nt(q_in[:, sl], st.astype(BF16))
        st_ref[h] = st * dec[:, sl] + _dot_tn(v[:, sl], k_st[:, sl])
        ms = jnp.mean(o_h * o_h, axis=-1, keepdims=True)
        outs.append(o_h * lax.rsqrt(ms + RMS_EPS))
    o = jnp.concatenate(outs, axis=-1)
    g = g_ref[...].astype(F32)
    o_ref[...] = (o * nw_ref[...] * (g * _sigmoid(g))).astype(BF16)


def _hgrn2(qig, f_pre, lb, norm_w, bsz, seq):
    T, dh = f_pre.shape
    heads = dh // HGRN_HEAD_DIM
    C = min(HGRN_CHUNK, seq)
    wcat_np, mask_np, nlev = _hgrn_tables(C)
    nc = seq // C
    lbp = jnp.stack([jnp.log(lb), jnp.log1p(-lb), 1.0 - lb], axis=0)
    lbp = jnp.concatenate([lbp, jnp.zeros((5, dh), F32)], axis=0)
    nw = jnp.tile(norm_w.astype(F32), heads).reshape(1, dh)
    row = lambda b, c: b * nc + c
    kern = functools.partial(_hgrn2_kernel, C=C, nlev=nlev, heads=heads)
    return pl.pallas_call(
        kern,
        grid=(bsz, nc),
        in_specs=[
            pl.BlockSpec((C, dh), lambda b, c: (row(b, c), 0)),
            pl.BlockSpec((C, dh), lambda b, c: (row(b, c), 1)),
            pl.BlockSpec((C, dh), lambda b, c: (row(b, c), 2)),
            pl.BlockSpec((C, dh), lambda b, c: (row(b, c), 0)),
            pl.BlockSpec((8, dh), lambda b, c: (0, 0)),
            pl.BlockSpec((1, dh), lambda b, c: (0, 0)),
            pl.BlockSpec(wcat_np.shape, lambda b, c: (0, 0)),
            pl.BlockSpec(mask_np.shape, lambda b, c: (0, 0, 0)),
        ],
        out_specs=pl.BlockSpec((C, dh), lambda b, c: (row(b, c), 0)),
        out_shape=jax.ShapeDtypeStruct((T, dh), BF16),
        scratch_shapes=[pltpu.VMEM((heads, HGRN_HEAD_DIM, HGRN_HEAD_DIM), F32)],
        compiler_params=_cparams(("arbitrary", "arbitrary")),
        name="hgrn2",
    )(qig, qig, qig, f_pre, lbp, nw, jnp.asarray(wcat_np, BF16), jnp.asarray(mask_np, F32))


def _s5_tables(a_re, a_im, log_dt, b_re, b_im, c_re, c_im, d_skip, L, nsteps):
    G, P = a_re.shape
    I = S5_GROUP
    nq = G // GROUPS_PER_TILE
    hp = lax.Precision.HIGHEST
    A = lax.complex(jnp.minimum(a_re.astype(F32), -S5_MIN_NEG), a_im.astype(F32))
    dt = jnp.exp(log_dt.astype(F32))[:, None]
    adt = A * dt
    a_bar = jnp.exp(adt)
    B = lax.complex(b_re.astype(F32), b_im.astype(F32))
    b_bar = ((a_bar - 1.0) / A)[..., None] * B
    Cc = lax.complex(c_re.astype(F32), c_im.astype(F32))
    tau = jnp.arange(L + 1, dtype=F32)
    apow = jnp.exp(adt[:, None, :] * tau[None, :, None])
    eye = jnp.eye(GROUPS_PER_TILE, dtype=F32)

    kt = jnp.einsum('gip,gtp,gpj->gtij', Cc, apow[:, :L], b_bar, precision=hp).real
    idx = jnp.arange(L)[None, :] - jnp.arange(L)[:, None]
    ktoep = jnp.where((idx >= 0)[None, :, :, None, None],
                      kt[:, jnp.clip(idx, 0, L - 1)], 0.0)
    ktoep = ktoep.reshape(nq, GROUPS_PER_TILE, L, L, I, I).transpose(0, 2, 1, 5, 3, 4)
    tmat = ktoep[:, :, :, :, :, None, :] * eye[None, None, :, None, None, :, None]
    tmat = tmat.reshape(nq, L * LANE, L * LANE).astype(BF16)

    bfull = apow[:, L - 1 - jnp.arange(L), :, None] * b_bar[:, None]
    bri = jnp.stack([bfull.real, bfull.imag], axis=0)
    bri = bri.reshape(2, nq, GROUPS_PER_TILE, L, P, I).transpose(1, 3, 2, 5, 0, 4)
    bmat = bri[:, :, :, :, :, None, :] * eye[None, None, :, None, None, :, None]
    bmat = bmat.reshape(nq, L * LANE, 2 * STATE_PER_TILE).astype(BF16)

    cfull = Cc[:, None] * apow[:, 1:, None, :]
    cri = jnp.stack([cfull.real, -cfull.imag], axis=0)
    cri = cri.reshape(2, nq, GROUPS_PER_TILE, L, I, P).transpose(1, 0, 2, 5, 3, 4)
    cmat = cri[:, :, :, :, :, None, :] * eye[None, None, :, None, None, :, None]
    cmat = cmat.reshape(nq, 2 * STATE_PER_TILE, L * LANE).astype(BF16)

    steps = (L * (2.0 ** jnp.arange(nsteps, dtype=F32)))
    alp = jnp.exp(adt[None] * steps[:, None, None])
    alp = jnp.concatenate([alp.real.reshape(nsteps, nq, STATE_PER_TILE),
                           alp.imag.reshape(nsteps, nq, STATE_PER_TILE)], axis=-1)
    alp = alp.transpose(1, 0, 2)

    dflat = jnp.tile(d_skip.astype(F32).reshape(nq, 1, LANE), (1, L, 1)).reshape(nq, 1, L * LANE)
    return tmat, bmat, cmat, alp, dflat


def _s5_kernel(u_ref, t_ref, b_ref, c_ref, al_ref, d_ref, z_ref, carry_ref, *, R, nsteps):
    sp = STATE_PER_TILE

    @pl.when(pl.program_id(2) == 0)
    def _():
        carry_ref[...] = jnp.zeros_like(carry_ref)

    u = u_ref[0]
    z = _dot(u, b_ref[0])
    re = z[:, :sp]
    im = z[:, sp:]
    row = lax.broadcasted_iota(jnp.int32, (R, sp), 0)
    cre = carry_ref[:, :sp]
    cim = carry_ref[:, sp:]
    a_re = al_ref[0, 0:1, :sp]
    a_im = al_ref[0, 0:1, sp:]
    first = row == 0
    re = re + jnp.where(first, a_re * cre - a_im * cim, 0.0)
    im = im + jnp.where(first, a_re * cim + a_im * cre, 0.0)
    d = 1
    for k in range(nsteps):
        p_re = al_ref[0, k:k + 1, :sp]
        p_im = al_ref[0, k:k + 1, sp:]
        keep = row >= d
        s_re = jnp.where(keep, pltpu.roll(re, d, axis=0), 0.0)
        s_im = jnp.where(keep, pltpu.roll(im, d, axis=0), 0.0)
        re, im = re + p_re * s_re - p_im * s_im, im + p_re * s_im + p_im * s_re
        d *= 2
    h_re = jnp.where(first, cre, pltpu.roll(re, 1, axis=0))
    h_im = jnp.where(first, cim, pltpu.roll(im, 1, axis=0))
    carry_ref[:, :sp] = re[R - 1:R, :]
    carry_ref[:, sp:] = im[R - 1:R, :]
    hprev = jnp.concatenate([h_re, h_im], axis=-1).astype(BF16)
    y = _dot(u, t_ref[0]) + _dot(hprev, c_ref[0]) + d_ref[0] * u.astype(F32)
    zz = 0.5 * y * (1.0 + jnp.tanh(math.sqrt(2.0 / math.pi) * (y + 0.044715 * (y * y * y))))
    z_ref[0] = zz.astype(BF16)


def _s5(u_q, tables, bsz, seq):
    tmat, bmat, cmat, alp, dflat = tables
    nq, T, _ = u_q.shape
    L = tmat.shape[1] // LANE
    nb = seq // L
    R = min(S5_ROWS, nb)
    nrb = nb // R
    nsteps = alp.shape[1]
    u2 = u_q.reshape(nq, T // L, L * LANE)
    kern = functools.partial(_s5_kernel, R=R, nsteps=nsteps)
    z = pl.pallas_call(
        kern,
        grid=(nq, bsz, nrb),
        in_specs=[
            pl.BlockSpec((1, R, L * LANE), lambda q, b, r: (q, b * nrb + r, 0)),
            pl.BlockSpec((1, L * LANE, L * LANE), lambda q, b, r: (q, 0, 0)),
            pl.BlockSpec((1, L * LANE, 2 * STATE_PER_TILE), lambda q, b, r: (q, 0, 0)),
            pl.BlockSpec((1, 2 * STATE_PER_TILE, L * LANE), lambda q, b, r: (q, 0, 0)),
            pl.BlockSpec((1, nsteps, 2 * STATE_PER_TILE), lambda q, b, r: (q, 0, 0)),
            pl.BlockSpec((1, 1, L * LANE), lambda q, b, r: (q, 0, 0)),
        ],
        out_specs=pl.BlockSpec((1, R, L * LANE), lambda q, b, r: (q, b * nrb + r, 0)),
        out_shape=jax.ShapeDtypeStruct((nq, T // L, L * LANE), BF16),
        scratch_shapes=[pltpu.VMEM((1, 2 * STATE_PER_TILE), F32)],
        compiler_params=_cparams(("arbitrary", "arbitrary", "arbitrary")),
        name="s5",
    )(u2, tmat, bmat, cmat, alp, dflat)
    return z.reshape(nq, T, LANE)


def _merge_kernel(ya_ref, z_ref, ga_ref, gb_ref, wglu_ref, wa_ref, wb_ref, o_ref):
    zc = jnp.concatenate([z_ref[q] for q in range(z_ref.shape[0])], axis=-1)
    yb = (zc.astype(F32) * _sigmoid(_dot(zc, wglu_ref[...]))).astype(BF16)
    ga = _sigmoid(ga_ref[...].astype(F32))
    gb = _sigmoid(gb_ref[...].astype(F32))
    m = ga * _dot(ya_ref[...], wa_ref[...]) + gb * _dot(yb, wb_ref[...])
    o_ref[...] = m.astype(BF16)


def _merge(ya, z_q, gates, wglu, wa, wb):
    T, dh = ya.shape
    nq = z_q.shape[0]
    ds5 = nq * LANE
    D = wa.shape[1]
    tm = min(512, T)
    return pl.pallas_call(
        _merge_kernel,
        grid=(T // tm,),
        in_specs=[
            pl.BlockSpec((tm, dh), lambda i: (i, 0)),
            pl.BlockSpec((nq, tm, LANE), lambda i: (0, i, 0)),
            pl.BlockSpec((tm, D), lambda i: (i, 0)),
            pl.BlockSpec((tm, D), lambda i: (i, 1)),
            pl.BlockSpec((ds5, ds5), lambda i: (0, 0)),
            pl.BlockSpec((dh, D), lambda i: (0, 0)),
            pl.BlockSpec((ds5, D), lambda i: (0, 0)),
        ],
        out_specs=pl.BlockSpec((tm, D), lambda i: (i, 0)),
        out_shape=jax.ShapeDtypeStruct((T, D), BF16),
        compiler_params=_cparams(("arbitrary",)),
        name="merge",
    )(ya, z_q, gates, gates, wglu, wa, wb)


def _outproj_kernel(x_ref, m_ref, w_ref, o_ref):
    o_ref[...] = x_ref[...] + _dot(m_ref[...], w_ref[...])


def _outproj(x, merged, wout):
    T, D = x.shape
    tm = min(512, T)
    return pl.pallas_call(
        _outproj_kernel,
        grid=(T // tm,),
        in_specs=[
            pl.BlockSpec((tm, D), lambda i: (i, 0)),
            pl.BlockSpec((tm, D), lambda i: (i, 0)),
            pl.BlockSpec((D, D), lambda i: (0, 0)),
        ],
        out_specs=pl.BlockSpec((tm, D), lambda i: (i, 0)),
        out_shape=jax.ShapeDtypeStruct((T, D), F32),
        compiler_params=_cparams(("arbitrary",)),
        name="outproj",
    )(x, merged, wout)


def _dense_ffn_kernel(x_ref, nw_ref, wg_ref, wu_ref, wd_ref, o_ref, h_scr):
    @pl.when(pl.program_id(1) == 0)
    def _():
        x = x_ref[...]
        h_scr[...] = _rmsnorm(x, nw_ref[...]).astype(BF16)
        o_ref[...] = x

    h = h_scr[...]
    g = _dot(h, wg_ref[...])
    u = _dot(h, wu_ref[...])
    a = (g * _sigmoid(g) * u).astype(BF16)
    o_ref[...] += _dot(a, wd_ref[...])


def _dense_ffn(x, norm_w, wg, wu, wd):
    T, D = x.shape
    F = wg.shape[1]
    tm = min(1024, T)
    tf = 512 if F % 512 == 0 else F
    return pl.pallas_call(
        _dense_ffn_kernel,
        grid=(T // tm, F // tf),
        in_specs=[
            pl.BlockSpec((tm, D), lambda i, f: (i, 0)),
            pl.BlockSpec((1, D), lambda i, f: (0, 0)),
            pl.BlockSpec((D, tf), lambda i, f: (0, f)),
            pl.BlockSpec((D, tf), lambda i, f: (0, f)),
            pl.BlockSpec((tf, D), lambda i, f: (f, 0)),
        ],
        out_specs=pl.BlockSpec((tm, D), lambda i, f: (i, 0)),
        out_shape=jax.ShapeDtypeStruct((T, D), F32),
        scratch_shapes=[pltpu.VMEM((tm, D), BF16)],
        compiler_params=_cparams(("arbitrary", "arbitrary")),
        name="dense_ffn",
    )(x, norm_w.reshape(1, D), wg, wu, wd)


def _router_kernel(x_ref, nw_ref, whi_ref, wlo_ref, tri_ref, meta_ref, cnt_ref, run_ref, *, n_exp):
    i = pl.program_id(0)

    @pl.when(i == 0)
    def _():
        run_ref[...] = jnp.zeros_like(run_ref)

    h = _rmsnorm(x_ref[...], nw_ref[...])
    hi = h.astype(BF16)
    lo = (h - hi.astype(F32)).astype(BF16)
    logits = _dot(hi, whi_ref[...]) + _dot(lo, whi_ref[...]) + _dot(hi, wlo_ref[...])
    tm = logits.shape[0]
    lane = lax.broadcasted_iota(jnp.int32, (tm, LANE), 1)
    neg = jnp.float32(-jnp.inf)
    l1 = jnp.where(lane < n_exp, logits, neg)
    m1 = jnp.max(l1, axis=-1, keepdims=True)
    i1 = jnp.min(jnp.where(l1 == m1, lane, LANE), axis=-1, keepdims=True)
    l2 = jnp.where(lane == i1, neg, l1)
    m2 = jnp.max(l2, axis=-1, keepdims=True)
    i2 = jnp.min(jnp.where(l2 == m2, lane, LANE), axis=-1, keepdims=True)
    g1 = 1.0 / (1.0 + jnp.exp(m2 - m1))
    g2 = 1.0 - g1
    sel1 = lane == i1
    sel2 = lane == i2
    twohot = (sel1 | sel2).astype(F32)
    before = _dot(tri_ref[...], twohot.astype(BF16)) + run_ref[...]
    p1 = jnp.sum(jnp.where(sel1, before, 0.0), axis=-1, keepdims=True)
    p2 = jnp.sum(jnp.where(sel2, before, 0.0), axis=-1, keepdims=True)
    run = run_ref[...] + jnp.sum(twohot, axis=0, keepdims=True)
    run_ref[...] = run
    cnt_ref[...] = jnp.broadcast_to(run, cnt_ref.shape)
    meta = jnp.where(lane == 0, i1.astype(F32), 0.0)
    meta = jnp.where(lane == 1, i2.astype(F32), meta)
    meta = jnp.where(lane == 2, g1, meta)
    meta = jnp.where(lane == 3, g2, meta)
    meta = jnp.where(lane == 4, p1, meta)
    meta = jnp.where(lane == 5, p2, meta)
    meta_ref[...] = meta


def _router(x, norm_w, w_router):
    T, D = x.shape
    n_exp = w_router.shape[1]
    tm = min(512, T)
    wpad = jnp.zeros((D, LANE), F32).at[:, :n_exp].set(w_router.astype(F32))
    whi = wpad.astype(BF16)
    wlo = (wpad - whi.astype(F32)).astype(BF16)
    tri = jnp.asarray(np.tril(np.ones((tm, tm), np.float32), -1), BF16)
    kern = functools.partial(_router_kernel, n_exp=n_exp)
    meta, cnt = pl.pallas_call(
        kern,
        grid=(T // tm,),
        in_specs=[
            pl.BlockSpec((tm, D), lambda i: (i, 0)),
            pl.BlockSpec((1, D), lambda i: (0, 0)),
            pl.BlockSpec((D, LANE), lambda i: (0, 0)),
            pl.BlockSpec((D, LANE), lambda i: (0, 0)),
            pl.BlockSpec((tm, tm), lambda i: (0, 0)),
        ],
        out_specs=[
            pl.BlockSpec((tm, LANE), lambda i: (i, 0)),
            pl.BlockSpec((8, LANE), lambda i: (0, 0)),
        ],
        out_shape=[
            jax.ShapeDtypeStruct((T, LANE), F32),
            jax.ShapeDtypeStruct((8, LANE), F32),
        ],
        scratch_shapes=[pltpu.VMEM((1, LANE), F32)],
        compiler_params=_cparams(("arbitrary",)),
        name="router",
    )(x, norm_w.reshape(1, D), whi, wlo, tri)
    return meta, cnt[0, :n_exp]


def _zero_fill_pads(padlo_ref, padlen_ref, xs_ref, z_scr, sem, *, n_exp, blk, n_blocks):
    z_scr[...] = jnp.zeros_like(z_scr)
    sub = 8
    bits = [1 << k for k in reversed(range(3, int(math.log2(blk))))]

    def pad_copies(run):
        for e in range(n_exp):
            lo = padlo_ref[e]
            ln = padlen_ref[e]
            head = (sub - lo % sub) % sub
            for r in range(sub - 1):
                @pl.when(r < head)
                def _(r=r):
                    run(pltpu.make_async_copy(z_scr.at[pl.ds(0, 1)], xs_ref.at[pl.ds(lo + r, 1)], sem))

            off = lo + head
            rem = ln - head
            for sz in bits:
                take = (rem & sz) != 0

                @pl.when(take)
                def _(off=off, sz=sz):
                    dst = xs_ref.at[pl.ds(pl.multiple_of(off, sub), sz)]
                    run(pltpu.make_async_copy(z_scr.at[pl.ds(0, sz)], dst, sem))

                off = off + jnp.where(take, sz, 0)

        def tail(b, c):
            @pl.when(b * blk >= padlo_ref[n_exp])
            def _():
                dst = xs_ref.at[pl.ds(pl.multiple_of(b * blk, blk), blk)]
                run(pltpu.make_async_copy(z_scr, dst, sem))
            return c

        lax.fori_loop(0, n_blocks, tail, 0)

    pad_copies(lambda cp: cp.start())
    pad_copies(lambda cp: cp.wait())


def _dispatch_kernel(padlo_ref, padlen_ref, d1_ref, d2_ref, x_ref, nw_ref, xs_ref, h_scr, z_scr,
                     sem, zsem, *, tm, n_exp, blk, n_blocks):
    @pl.when(pl.program_id(0) == 0)
    def _():
        _zero_fill_pads(padlo_ref, padlen_ref, xs_ref, z_scr, zsem,
                        n_exp=n_exp, blk=blk, n_blocks=n_blocks)

    h_scr[...] = _rmsnorm(x_ref[...], nw_ref[...])

    def copy(t, dst):
        return pltpu.make_async_copy(h_scr.at[pl.ds(t, 1)], xs_ref.at[pl.ds(dst, 1)], sem)

    def start(t, c):
        copy(t, d1_ref[0, 0, t]).start()
        copy(t, d2_ref[0, 0, t]).start()
        return c

    lax.fori_loop(0, tm, start, 0)

    def wait(t, c):
        copy(t, d1_ref[0, 0, t]).wait()
        copy(t, d2_ref[0, 0, t]).wait()
        return c

    lax.fori_loop(0, tm, wait, 0)


def _dispatch(x, norm_w, dest1, dest2, pad_lo, pad_len, n_rows, blk):
    T, D = x.shape
    tm = min(256, T)
    nb = T // tm
    n_exp = pad_len.shape[0]
    kern = functools.partial(_dispatch_kernel, tm=tm, n_exp=n_exp, blk=blk, n_blocks=n_rows // blk)
    smem_spec = pl.BlockSpec((1, 1, tm), lambda i, lo, ln: (i, 0, 0), memory_space=pltpu.SMEM)
    grid_spec = pltpu.PrefetchScalarGridSpec(
        num_scalar_prefetch=2,
        grid=(nb,),
        in_specs=[
            smem_spec,
            smem_spec,
            pl.BlockSpec((tm, D), lambda i, lo, ln: (i, 0)),
            pl.BlockSpec((1, D), lambda i, lo, ln: (0, 0)),
        ],
        out_specs=pl.BlockSpec(memory_space=pl.ANY),
        scratch_shapes=[pltpu.VMEM((tm, D), F32), pltpu.VMEM((blk, D), F32),
                        pltpu.SemaphoreType.DMA(()), pltpu.SemaphoreType.DMA(())],
    )
    return pl.pallas_call(
        kern,
        grid_spec=grid_spec,
        out_shape=jax.ShapeDtypeStruct((n_rows, D), F32),
        compiler_params=_cparams(("arbitrary",)),
        name="dispatch",
    )(pad_lo, pad_len, dest1.reshape(nb, 1, tm), dest2.reshape(nb, 1, tm), x, norm_w.reshape(1, D))


def _expert_kernel(exp_ref, nv_ref, xs_ref, wg_ref, wu_ref, wd_ref, y_ref, xb_scr):
    b = pl.program_id(0)
    nv = nv_ref[b]

    @pl.when(pl.program_id(1) == 0)
    def _():
        xb_scr[...] = xs_ref[...].astype(BF16)
        y_ref[...] = jnp.zeros_like(y_ref)

    @pl.when(nv > 0)
    def _():
        xb = xb_scr[...]
        g = _dot(xb, wg_ref[0])
        u = _dot(xb, wu_ref[0])
        a = (g * _sigmoid(g) * u).astype(BF16)
        y_ref[...] += _dot(a, wd_ref[0])


def _experts(xs, blk_exp, blk_nv, wg, wu, wd):
    n_rows, D = xs.shape
    F = wg.shape[2]
    tm = MOE_BLOCK
    tf = 512 if F % 512 == 0 else F
    grid_spec = pltpu.PrefetchScalarGridSpec(
        num_scalar_prefetch=2,
        grid=(n_rows // tm, F // tf),
        in_specs=[
            pl.BlockSpec((tm, D), lambda b, f, e, n: (b, 0)),
            pl.BlockSpec((1, D, tf), lambda b, f, e, n: (e[b], 0, f)),
            pl.BlockSpec((1, D, tf), lambda b, f, e, n: (e[b], 0, f)),
            pl.BlockSpec((1, tf, D), lambda b, f, e, n: (e[b], f, 0)),
        ],
        out_specs=pl.BlockSpec((tm, D), lambda b, f, e, n: (b, 0)),
        scratch_shapes=[pltpu.VMEM((tm, D), BF16)],
    )
    return pl.pallas_call(
        _expert_kernel,
        grid_spec=grid_spec,
        out_shape=jax.ShapeDtypeStruct((n_rows, D), F32),
        compiler_params=_cparams(("arbitrary", "arbitrary")),
        name="experts",
    )(blk_exp, blk_nv, xs, wg, wu, wd)


def _combine_kernel(d1_ref, d2_ref, x_ref, meta_ref, nw_ref, y_ref, o_ref, ybuf, sem, *, tm, final_norm):
    def copy(t, k, src):
        return pltpu.make_async_copy(y_ref.at[pl.ds(src, 1)], ybuf.at[k, pl.ds(t, 1)], sem)

    def start(t, c):
        copy(t, 0, d1_ref[0, 0, t]).start()
        copy(t, 1, d2_ref[0, 0, t]).start()
        return c

    lax.fori_loop(0, tm, start, 0)

    def wait(t, c):
        copy(t, 0, d1_ref[0, 0, t]).wait()
        copy(t, 1, d2_ref[0, 0, t]).wait()
        return c

    lax.fori_loop(0, tm, wait, 0)
    meta = meta_ref[...]
    g1 = meta[:, 2:3]
    g2 = meta[:, 3:4]
    out = x_ref[...] + (g1 * ybuf[0] + g2 * ybuf[1])
    if final_norm:
        out = _rmsnorm(out, nw_ref[...])
    o_ref[...] = out


def _combine(x, y, meta, dest1, dest2, final_w):
    T, D = x.shape
    tm = min(256, T)
    nb = T // tm
    final_norm = final_w is not None
    nw = (final_w if final_norm else jnp.ones((D,), F32)).reshape(1, D)
    kern = functools.partial(_combine_kernel, tm=tm, final_norm=final_norm)
    smem_spec = pl.BlockSpec((1, 1, tm), lambda i: (i, 0, 0), memory_space=pltpu.SMEM)
    return pl.pallas_call(
        kern,
        grid=(nb,),
        in_specs=[
            smem_spec,
            smem_spec,
            pl.BlockSpec((tm, D), lambda i: (i, 0)),
            pl.BlockSpec((tm, LANE), lambda i: (i, 0)),
            pl.BlockSpec((1, D), lambda i: (0, 0)),
            pl.BlockSpec(memory_space=pl.ANY),
        ],
        out_specs=pl.BlockSpec((tm, D), lambda i: (i, 0)),
        out_shape=jax.ShapeDtypeStruct((T, D), F32),
        scratch_shapes=[pltpu.VMEM((2, tm, D), F32), pltpu.SemaphoreType.DMA(())],
        compiler_params=_cparams(("arbitrary",)),
        name="combine",
    )(dest1.reshape(nb, 1, tm), dest2.reshape(nb, 1, tm), x, meta, nw, y)


def _moe(x, norm_w, w_router, wg, wu, wd, final_w):
    T, D = x.shape
    n_exp = w_router.shape[1]
    blk = MOE_BLOCK
    meta, counts_f = _router(x, norm_w, w_router)
    counts = counts_f.astype(jnp.int32)
    padded = (counts + blk - 1) // blk * blk
    pad_ends = jnp.cumsum(padded)
    pad_starts = pad_ends - padded
    n_rows = (T * TOP_K + blk - 1) // blk * blk + n_exp * blk
    n_blocks = n_rows // blk
    e1 = meta[:, 0].astype(jnp.int32)
    e2 = meta[:, 1].astype(jnp.int32)
    dest1 = pad_starts[e1] + meta[:, 4].astype(jnp.int32)
    dest2 = pad_starts[e2] + meta[:, 5].astype(jnp.int32)
    blk_start = jnp.arange(n_blocks, dtype=jnp.int32) * blk
    blk_exp = jnp.minimum(jnp.searchsorted(pad_ends, blk_start, side='right'),
                          n_exp - 1).astype(jnp.int32)
    blk_nv = jnp.clip(pad_starts[blk_exp] + counts[blk_exp] - blk_start, 0, blk).astype(jnp.int32)
    pad_lo = jnp.concatenate([pad_starts + counts, pad_ends[-1:]]).astype(jnp.int32)
    pad_len = (padded - counts).astype(jnp.int32)
    xs = _dispatch(x, norm_w, dest1, dest2, pad_lo, pad_len, n_rows, blk)
    y = _experts(xs, blk_exp, blk_nv, wg, wu, wd)
    return _combine(x, y, meta, dest1, dest2, final_w)


def _final_norm_kernel(x_ref, nw_ref, o_ref):
    o_ref[...] = _rmsnorm(x_ref[...], nw_ref[...])


def _final_norm(x, w):
    T, D = x.shape
    tm = min(1024, T)
    return pl.pallas_call(
        _final_norm_kernel,
        grid=(T // tm,),
        in_specs=[pl.BlockSpec((tm, D), lambda i: (i, 0)), pl.BlockSpec((1, D), lambda i: (0, 0))],
        out_specs=pl.BlockSpec((tm, D), lambda i: (i, 0)),
        out_shape=jax.ShapeDtypeStruct((T, D), F32),
        compiler_params=_cparams(("arbitrary",)),
        name="final_norm",
    )(x, w.reshape(1, D))


def kernel(x, attn_norm_w, w_in, hgrn_lb_logits, hgrn_norm_w, s5_a_re, s5_a_im, s5_log_dt,
           s5_b_re, s5_b_im, s5_c_re, s5_c_im, s5_d, s5_w_glu, w_branch_a, w_branch_b,
           w_out, ffn_norm_w, dense_w_gate, dense_w_up, dense_w_down, moe_w_router,
           moe_w_gate, moe_w_up, moe_w_down, final_norm_w):
    bsz, seq, D = x.shape
    depth = w_in.shape[0]
    dh = hgrn_lb_logits.shape[1]
    T = bsz * seq
    lower = jnp.cumsum(jax.nn.softmax(hgrn_lb_logits.astype(F32), axis=0), axis=0)
    lower = lower - lower[0]
    nb = seq // S5_L
    nsteps = max(1, int(math.log2(min(S5_ROWS, nb))))
    xf = x.reshape(T, D).astype(F32)
    for layer in range(depth):
        w = w_in[layer]
        seg = lambda k, n=1: w[:, k * dh:(k + n) * dh]
        w_r = jnp.concatenate([seg(0), seg(2), seg(3), seg(5, 2), seg(7, 2), seg(1), seg(4)],
                              axis=1).astype(BF16)
        qig, gates, f_pre, u_q = _norm_inproj(xf, attn_norm_w[layer], w_r, dh)
        ya = _hgrn2(qig, f_pre, lower[layer], hgrn_norm_w[layer], bsz, seq)
        tables = _s5_tables(s5_a_re[layer], s5_a_im[layer], s5_log_dt[layer], s5_b_re[layer],
                            s5_b_im[layer], s5_c_re[layer], s5_c_im[layer], s5_d[layer],
                            S5_L, nsteps)
        z_q = _s5(u_q, tables, bsz, seq)
        merged = _merge(ya, z_q, gates, s5_w_glu[layer].astype(BF16),
                        w_branch_a[layer].astype(BF16), w_branch_b[layer].astype(BF16))
        xf = _outproj(xf, merged, w_out[layer].astype(BF16))
        last = layer == depth - 1
        j = layer // 2
        if layer % 2 == 0:
            xf = _dense_ffn(xf, ffn_norm_w[layer], dense_w_gate[j].astype(BF16),
                            dense_w_up[j].astype(BF16), dense_w_down[j].astype(BF16))
            if last:
                xf = _final_norm(xf, final_norm_w)
        else:
            xf = _moe(xf, ffn_norm_w[layer], moe_w_router[j], moe_w_gate[j].astype(BF16),
                      moe_w_up[j].astype(BF16), moe_w_down[j].astype(BF16),
                      final_norm_w if last else None)
    return xf.reshape(bsz, seq, D).astype(x.dtype)
```

```python
import functools
import math

import numpy as np
import jax
import jax.numpy as jnp
from jax import lax
from jax.experimental import pallas as pl
from jax.experimental.pallas import tpu as pltpu

F32 = jnp.float32
BF16 = jnp.bfloat16

RMS_EPS = 1e-6
HGRN_HEAD_DIM = 128
S5_GROUP = 16
S5_STATE = 64
S5_MIN_NEG = 1e-4
TOP_K = 2

LANE = 128
VMEM_LIMIT = 56 * 1024 * 1024

HGRN_CHUNK = 128
S5_L = 16
S5_ROWS = 512
MOE_BLOCK = 1024
GROUPS_PER_TILE = LANE // S5_GROUP
STATE_PER_TILE = GROUPS_PER_TILE * S5_STATE


def _cparams(sem):
    return pltpu.CompilerParams(dimension_semantics=sem, vmem_limit_bytes=VMEM_LIMIT)


def _dot(a, b):
    return jnp.dot(a, b, preferred_element_type=F32)


def _dot_nt(a, b):
    return lax.dot_general(a, b, (((1,), (1,)), ((), ())), preferred_element_type=F32)


def _dot_tn(a, b):
    return lax.dot_general(a, b, (((0,), (0,)), ((), ())), preferred_element_type=F32)


def _sigmoid(x):
    return 1.0 / (1.0 + jnp.exp(-x))


def _rmsnorm(x, w):
    ms = jnp.mean(x * x, axis=-1, keepdims=True)
    return x * lax.rsqrt(ms + RMS_EPS) * w


def _norm_inproj_kernel(x_ref, nw_ref, w_ref, qig_ref, gates_ref, f_ref, u_ref, h_scr):
    j = pl.program_id(1)

    @pl.when(j == 0)
    def _():
        h_scr[...] = _rmsnorm(x_ref[...], nw_ref[...]).astype(BF16)

    acc = _dot(h_scr[...], w_ref[...])

    @pl.when(j < 3)
    def _():
        qig_ref[...] = acc.astype(BF16)

    @pl.when((j >= 3) & (j < 7))
    def _():
        gates_ref[...] = acc.astype(BF16)

    @pl.when(j == 7)
    def _():
        f_ref[...] = acc

    @pl.when(j == 8)
    def _():
        for q in range(u_ref.shape[0]):
            u_ref[q] = acc[:, q * LANE:(q + 1) * LANE].astype(BF16)


def _norm_inproj(x, norm_w, w_r, dh):
    T, D = x.shape
    tm = min(1024, T)
    nq = dh // LANE
    grid = (T // tm, 9)
    return pl.pallas_call(
        _norm_inproj_kernel,
        grid=grid,
        in_specs=[
            pl.BlockSpec((tm, D), lambda i, j: (i, 0)),
            pl.BlockSpec((1, D), lambda i, j: (0, 0)),
            pl.BlockSpec((D, dh), lambda i, j: (0, j)),
        ],
        out_specs=[
            pl.BlockSpec((tm, dh), lambda i, j: (i, jnp.minimum(j, 2))),
            pl.BlockSpec((tm, dh), lambda i, j: (i, jnp.clip(j - 3, 0, 3))),
            pl.BlockSpec((tm, dh), lambda i, j: (i, 0)),
            pl.BlockSpec((nq, tm, LANE), lambda i, j: (0, i, 0)),
        ],
        out_shape=[
            jax.ShapeDtypeStruct((T, 3 * dh), BF16),
            jax.ShapeDtypeStruct((T, 4 * dh), BF16),
            jax.ShapeDtypeStruct((T, dh), F32),
            jax.ShapeDtypeStruct((nq, T, LANE), BF16),
        ],
        scratch_shapes=[pltpu.VMEM((tm, D), BF16)],
        compiler_params=_cparams(("arbitrary", "arbitrary")),
        name="norm_inproj",
    )(x, norm_w.reshape(1, D), w_r)


def _hgrn_tables(C):
    nlev = int(math.log2(C))
    r = np.arange(C)[:, None]
    c = np.arange(C)[None, :]
    mats = [(c <= r), (c > r)]
    masks = [np.eye(C, dtype=bool)]
    for lev in range(1, nlev + 1):
        b = 2 ** lev
        half = b // 2
        mid = (r // b) * b + half
        upper = r >= mid
        mats.append(np.where(upper, (c >= mid) & (c <= r), (c > r) & (c < mid)))
        masks.append(((r // b) == (c // b)) & ((r % b) >= half) & ((c % b) < half))
    w = np.concatenate(mats, axis=0).astype(np.float32)
    wcat = np.concatenate([w, w], axis=1)
    m = np.stack(masks, axis=0).astype(np.float32)
    return wcat, m, nlev


def _hgrn2_kernel(q_ref, i_ref, g_ref, f_ref, lb_ref, nw_ref, wcat_ref, mask_ref,
                  o_ref, st_ref, *, C, nlev, heads):
    hd = HGRN_HEAD_DIM

    @pl.when(pl.program_id(1) == 0)
    def _():
        st_ref[...] = jnp.zeros_like(st_ref)

    x = f_ref[...]
    e = jnp.exp(-jnp.abs(x))
    log_sig = jnp.minimum(x, 0.0) - jnp.log(1.0 + e)
    a_ = lb_ref[0:1, :]
    b_ = lb_ref[1:2, :] + log_sig
    logf = jnp.maximum(a_, b_) + jnp.log(1.0 + jnp.exp(-jnp.abs(a_ - b_)))
    kk = lb_ref[2:3, :] * jnp.where(x >= 0.0, e, 1.0) / (1.0 + e)
    q = q_ref[...].astype(F32)
    qs = q * _sigmoid(q)
    v = i_ref[...]

    hi = logf.astype(BF16)
    lo = (logf - hi.astype(F32)).astype(BF16)
    hl = jnp.concatenate([hi, lo], axis=0)

    def window(idx):
        return jnp.exp(_dot(wcat_ref[idx * C:(idx + 1) * C, :], hl))

    e_q = window(0)
    q_in = (qs * e_q).astype(BF16)
    dec = e_q[C - 1:C, :]
    k_st = (kk * window(1)).astype(BF16)

    scores = [None] * heads
    for lev in range(nlev + 1):
        if lev == 0:
            ql = qs.astype(BF16)
            kl = kk.astype(BF16)
        else:
            e_l = window(1 + lev)
            ql = (qs * e_l).astype(BF16)
            kl = (kk * e_l).astype(BF16)
        m = mask_ref[lev]
        for h in range(heads):
            sl = slice(h * hd, (h + 1) * hd)
            p = _dot_nt(ql[:, sl], kl[:, sl]) * m
            scores[h] = p if scores[h] is None else scores[h] + p

    outs = []
    for h in range(heads):
        sl = slice(h * hd, (h + 1) * hd)
        st = st_ref[h]
        o_h = _dot(scores[h].astype(BF16), v[:, sl]) + _dot_nt(q_in[:, sl], st.astype(BF16))
        st_ref[h] = st * dec[:, sl] + _dot_tn(v[:, sl], k_st[:, sl])
        ms = jnp.mean(o_h * o_h, axis=-1, keepdims=True)
        outs.append(o_h * lax.rsqrt(ms + RMS_EPS))
    o = jnp.concatenate(outs, axis=-1)
    g = g_ref[...].astype(F32)
    o_ref[...] = (o * nw_ref[...] * (g * _sigmoid(g))).astype(BF16)


def _hgrn2(qig, f_pre, lb, norm_w, bsz, seq):
    T, dh = f_pre.shape
    heads = dh // HGRN_HEAD_DIM
    C = min(HGRN_CHUNK, seq)
    wcat_np, mask_np, nlev = _hgrn_tables(C)
    nc = seq // C
    lbp = jnp.stack([jnp.log(lb), jnp.log1p(-lb), 1.0 - lb], axis=0)
    lbp = jnp.concatenate([lbp, jnp.zeros((5, dh), F32)], axis=0)
    nw = jnp.tile(norm_w.astype(F32), heads).reshape(1, dh)
    row = lambda b, c: b * nc + c
    kern = functools.partial(_hgrn2_kernel, C=C, nlev=nlev, heads=heads)
    return pl.pallas_call(
        kern,
        grid=(bsz, nc),
        in_specs=[
            pl.BlockSpec((C, dh), lambda b, c: (row(b, c), 0)),
            pl.BlockSpec((C, dh), lambda b, c: (row(b, c), 1)),
            pl.BlockSpec((C, dh), lambda b, c: (row(b, c), 2)),
            pl.BlockSpec((C, dh), lambda b, c: (row(b, c), 0)),
            pl.BlockSpec((8, dh), lambda b, c: (0, 0)),
            pl.BlockSpec((1, dh), lambda b, c: (0, 0)),
            pl.BlockSpec(wcat_np.shape, lambda b, c: (0, 0)),
            pl.BlockSpec(mask_np.shape, lambda b, c: (0, 0, 0)),
        ],
        out_specs=pl.BlockSpec((C, dh), lambda b, c: (row(b, c), 0)),
        out_shape=jax.ShapeDtypeStruct((T, dh), BF16),
        scratch_shapes=[pltpu.VMEM((heads, HGRN_HEAD_DIM, HGRN_HEAD_DIM), F32)],
        compiler_params=_cparams(("arbitrary", "arbitrary")),
        name="hgrn2",
    )(qig, qig, qig, f_pre, lbp, nw, jnp.asarray(wcat_np, BF16), jnp.asarray(mask_np, F32))


def _s5_tables(a_re, a_im, log_dt, b_re, b_im, c_re, c_im, d_skip, L, nsteps):
    G, P = a_re.shape
    I = S5_GROUP
    gpt = GROUPS_PER_TILE
    nq = G // gpt
    sp = STATE_PER_TILE
    hp = lax.Precision.HIGHEST
    A = lax.complex(jnp.minimum(a_re.astype(F32), -S5_MIN_NEG), a_im.astype(F32))
    dt = jnp.exp(log_dt.astype(F32))[:, None]
    adt = A * dt
    a_bar = jnp.exp(adt)
    B = lax.complex(b_re.astype(F32), b_im.astype(F32))
    b_bar = ((a_bar - 1.0) / A)[..., None] * B
    Cc = lax.complex(c_re.astype(F32), c_im.astype(F32))
    tau = jnp.arange(L + 1, dtype=F32)
    apow = jnp.exp(adt[:, None, :] * tau[None, :, None])
    eye = jnp.eye(gpt, dtype=F32)

    def ri_lanes(z):
        lead = z.shape[:-2]
        return jnp.concatenate([z.real.reshape(*lead, nq, sp), z.imag.reshape(*lead, nq, sp)], axis=-1)

    kt = jnp.einsum('gip,gtp,gpj->gtij', Cc, apow[:, :L], b_bar, precision=hp).real
    kt = kt.reshape(nq, gpt, L, I, I).transpose(0, 2, 1, 4, 3)
    kbd = kt[:, :, :, :, None, :] * eye[None, None, :, None, :, None]
    kbd = kbd.reshape(nq, L, LANE, LANE).transpose(0, 2, 1, 3).reshape(nq, LANE, L * LANE)
    kstrip = jnp.concatenate([jnp.zeros((nq, LANE, (L - 1) * LANE), F32), kbd], axis=-1).astype(BF16)

    bb = b_bar.reshape(nq, gpt, P, I).transpose(0, 1, 3, 2)
    bb = bb[:, :, :, None, :] * eye[None, :, None, :, None]
    bbase = jnp.concatenate([bb.real.reshape(nq, LANE, sp), bb.imag.reshape(nq, LANE, sp)], axis=-1)
    cc = Cc.reshape(nq, gpt, I, P).transpose(0, 1, 3, 2)
    cc = cc[:, :, :, None, :] * eye[None, :, None, :, None]
    cbase = jnp.concatenate([cc.real.reshape(nq, sp, LANE), cc.imag.reshape(nq, sp, LANE)], axis=1)

    aprow = ri_lanes(apow[:, L - 1 - jnp.arange(L)].transpose(1, 0, 2)).transpose(1, 0, 2)
    apc = ri_lanes(apow[:, 1:].transpose(1, 0, 2)).transpose(1, 2, 0)
    apcol = jnp.concatenate([apc, jnp.zeros((nq, 2 * sp, LANE - L), F32)], axis=-1)

    steps = (L * (2.0 ** jnp.arange(nsteps, dtype=F32)))
    alp = ri_lanes(jnp.exp(adt[None] * steps[:, None, None])).transpose(1, 0, 2)

    dflat = jnp.tile(d_skip.astype(F32).reshape(nq, 1, LANE), (1, L, 1)).reshape(nq, 1, L * LANE)
    return kstrip, bbase, cbase, aprow, apcol, alp, dflat


def _s5_expand(k_ref, bb_ref, cb_ref, ar_ref, ac_ref, t_scr, b_scr, c_scr, *, L):
    sp = STATE_PER_TILE
    br = bb_ref[0, :, :sp]
    bi = bb_ref[0, :, sp:]
    cr = cb_ref[0, :sp, :]
    ci = cb_ref[0, sp:, :]
    for t in range(L):
        rows = slice(t * LANE, (t + 1) * LANE)
        off = (L - 1 - t) * LANE
        t_scr[rows, :] = k_ref[0, :, off:off + L * LANE]
        ar = ar_ref[0, t:t + 1, :sp]
        ai = ar_ref[0, t:t + 1, sp:]
        b_scr[rows, :sp] = (br * ar - bi * ai).astype(BF16)
        b_scr[rows, sp:] = (br * ai + bi * ar).astype(BF16)
        acr = ac_ref[0, :sp, t:t + 1]
        aci = ac_ref[0, sp:, t:t + 1]
        c_scr[:sp, rows] = (cr * acr - ci * aci).astype(BF16)
        c_scr[sp:, rows] = (-(cr * aci + ci * acr)).astype(BF16)


def _s5_kernel(u_ref, k_ref, bb_ref, cb_ref, ar_ref, ac_ref, al_ref, d_ref, z_ref,
               carry_ref, t_scr, b_scr, c_scr, *, R, L, nsteps):
    sp = STATE_PER_TILE

    @pl.when((pl.program_id(1) == 0) & (pl.program_id(2) == 0))
    def _():
        _s5_expand(k_ref, bb_ref, cb_ref, ar_ref, ac_ref, t_scr, b_scr, c_scr, L=L)

    @pl.when(pl.program_id(2) == 0)
    def _():
        carry_ref[...] = jnp.zeros_like(carry_ref)

    u = u_ref[0]
    z = _dot(u, b_scr[...])
    re = z[:, :sp]
    im = z[:, sp:]
    row = lax.broadcasted_iota(jnp.int32, (R, sp), 0)
    cre = carry_ref[:, :sp]
    cim = carry_ref[:, sp:]
    a_re = al_ref[0, 0:1, :sp]
    a_im = al_ref[0, 0:1, sp:]
    first = row == 0
    re = re + jnp.where(first, a_re * cre - a_im * cim, 0.0)
    im = im + jnp.where(first, a_re * cim + a_im * cre, 0.0)
    d = 1
    for k in range(nsteps):
        p_re = al_ref[0, k:k + 1, :sp]
        p_im = al_ref[0, k:k + 1, sp:]
        keep = row >= d
        s_re = jnp.where(keep, pltpu.roll(re, d, axis=0), 0.0)
        s_im = jnp.where(keep, pltpu.roll(im, d, axis=0), 0.0)
        re, im = re + p_re * s_re - p_im * s_im, im + p_re * s_im + p_im * s_re
        d *= 2
    h_re = jnp.where(first, cre, pltpu.roll(re, 1, axis=0))
    h_im = jnp.where(first, cim, pltpu.roll(im, 1, axis=0))
    carry_ref[:, :sp] = re[R - 1:R, :]
    carry_ref[:, sp:] = im[R - 1:R, :]
    hprev = jnp.concatenate([h_re, h_im], axis=-1).astype(BF16)
    y = _dot(u, t_scr[...]) + _dot(hprev, c_scr[...]) + d_ref[0] * u.astype(F32)
    zz = 0.5 * y * (1.0 + jnp.tanh(math.sqrt(2.0 / math.pi) * (y + 0.044715 * (y * y * y))))
    z_ref[0] = zz.astype(BF16)


def _s5(u_q, tables, bsz, seq):
    kstrip, bbase, cbase, aprow, apcol, alp, dflat = tables
    nq, T, _ = u_q.shape
    L = aprow.shape[1]
    sp2 = 2 * STATE_PER_TILE
    nb = seq // L
    R = min(S5_ROWS, nb)
    nrb = nb // R
    nsteps = alp.shape[1]
    u2 = u_q.reshape(nq, T // L, L * LANE)
    kern = functools.partial(_s5_kernel, R=R, L=L, nsteps=nsteps)
    per_q = lambda shape: pl.BlockSpec((1,) + shape, lambda q, b, r: (q, 0, 0))
    z = pl.pallas_call(
        kern,
        grid=(nq, bsz, nrb),
        in_specs=[
            pl.BlockSpec((1, R, L * LANE), lambda q, b, r: (q, b * nrb + r, 0)),
            per_q((LANE, (2 * L - 1) * LANE)),
            per_q((LANE, sp2)),
            per_q((sp2, LANE)),
            per_q((L, sp2)),
            per_q((sp2, LANE)),
            per_q((nsteps, sp2)),
            per_q((1, L * LANE)),
        ],
        out_specs=pl.BlockSpec((1, R, L * LANE), lambda q, b, r: (q, b * nrb + r, 0)),
        out_shape=jax.ShapeDtypeStruct((nq, T // L, L * LANE), BF16),
        scratch_shapes=[
            pltpu.VMEM((1, sp2), F32),
            pltpu.VMEM((L * LANE, L * LANE), BF16),
            pltpu.VMEM((L * LANE, sp2), BF16),
            pltpu.VMEM((sp2, L * LANE), BF16),
        ],
        compiler_params=_cparams(("arbitrary", "arbitrary", "arbitrary")),
        name="s5",
    )(u2, kstrip, bbase, cbase, aprow, apcol, alp, dflat)
    return z.reshape(nq, T, LANE)


def _merge_kernel(ya_ref, z_ref, ga_ref, gb_ref, wglu_ref, wa_ref, wb_ref, o_ref):
    zc = jnp.concatenate([z_ref[q] for q in range(z_ref.shape[0])], axis=-1)
    yb = (zc.astype(F32) * _sigmoid(_dot(zc, wglu_ref[...]))).astype(BF16)
    ga = _sigmoid(ga_ref[...].astype(F32))
    gb = _sigmoid(gb_ref[...].astype(F32))
    m = ga * _dot(ya_ref[...], wa_ref[...]) + gb * _dot(yb, wb_ref[...])
    o_ref[...] = m.astype(BF16)


def _merge(ya, z_q, gates, wglu, wa, wb):
    T, dh = ya.shape
    nq = z_q.shape[0]
    ds5 = nq * LANE
    D = wa.shape[1]
    tm = min(512, T)
    return pl.pallas_call(
        _merge_kernel,
        grid=(T // tm,),
        in_specs=[
            pl.BlockSpec((tm, dh), lambda i: (i, 0)),
            pl.BlockSpec((nq, tm, LANE), lambda i: (0, i, 0)),
            pl.BlockSpec((tm, D), lambda i: (i, 0)),
            pl.BlockSpec((tm, D), lambda i: (i, 1)),
            pl.BlockSpec((ds5, ds5), lambda i: (0, 0)),
            pl.BlockSpec((dh, D), lambda i: (0, 0)),
            pl.BlockSpec((ds5, D), lambda i: (0, 0)),
        ],
        out_specs=pl.BlockSpec((tm, D), lambda i: (i, 0)),
        out_shape=jax.ShapeDtypeStruct((T, D), BF16),
        compiler_params=_cparams(("arbitrary",)),
        name="merge",
    )(ya, z_q, gates, gates, wglu, wa, wb)


def _outproj_kernel(x_ref, m_ref, w_ref, o_ref):
    o_ref[...] = x_ref[...] + _dot(m_ref[...], w_ref[...])


def _outproj(x, merged, wout):
    T, D = x.shape
    tm = min(512, T)
    return pl.pallas_call(
        _outproj_kernel,
        grid=(T // tm,),
        in_specs=[
            pl.BlockSpec((tm, D), lambda i: (i, 0)),
            pl.BlockSpec((tm, D), lambda i: (i, 0)),
            pl.BlockSpec((D, D), lambda i: (0, 0)),
        ],
        out_specs=pl.BlockSpec((tm, D), lambda i: (i, 0)),
        out_shape=jax.ShapeDtypeStruct((T, D), F32),
        compiler_params=_cparams(("arbitrary",)),
        name="outproj",
    )(x, merged, wout)


def _dense_ffn_kernel(x_ref, nw_ref, wg_ref, wu_ref, wd_ref, o_ref, h_scr):
    @pl.when(pl.program_id(1) == 0)
    def _():
        x = x_ref[...]
        h_scr[...] = _rmsnorm(x, nw_ref[...]).astype(BF16)
        o_ref[...] = x

    h = h_scr[...]
    g = _dot(h, wg_ref[...])
    u = _dot(h, wu_ref[...])
    a = (g * _sigmoid(g) * u).astype(BF16)
    o_ref[...] += _dot(a, wd_ref[...])


def _dense_ffn(x, norm_w, wg, wu, wd):
    T, D = x.shape
    F = wg.shape[1]
    tm = min(1024, T)
    tf = 512 if F % 512 == 0 else F
    return pl.pallas_call(
        _dense_ffn_kernel,
        grid=(T // tm, F // tf),
        in_specs=[
            pl.BlockSpec((tm, D), lambda i, f: (i, 0)),
            pl.BlockSpec((1, D), lambda i, f: (0, 0)),
            pl.BlockSpec((D, tf), lambda i, f: (0, f)),
            pl.BlockSpec((D, tf), lambda i, f: (0, f)),
            pl.BlockSpec((tf, D), lambda i, f: (f, 0)),
        ],
        out_specs=pl.BlockSpec((tm, D), lambda i, f: (i, 0)),
        out_shape=jax.ShapeDtypeStruct((T, D), F32),
        scratch_shapes=[pltpu.VMEM((tm, D), BF16)],
        compiler_params=_cparams(("arbitrary", "arbitrary")),
        name="dense_ffn",
    )(x, norm_w.reshape(1, D), wg, wu, wd)


def _router_kernel(x_ref, nw_ref, whi_ref, wlo_ref, tri_ref, meta_ref, cnt_ref, run_ref, *, n_exp):
    i = pl.program_id(0)

    @pl.when(i == 0)
    def _():
        run_ref[...] = jnp.zeros_like(run_ref)

    h = _rmsnorm(x_ref[...], nw_ref[...])
    hi = h.astype(BF16)
    lo = (h - hi.astype(F32)).astype(BF16)
    logits = _dot(hi, whi_ref[...]) + _dot(lo, whi_ref[...]) + _dot(hi, wlo_ref[...])
    tm = logits.shape[0]
    lane = lax.broadcasted_iota(jnp.int32, (tm, LANE), 1)
    neg = jnp.float32(-jnp.inf)
    l1 = jnp.where(lane < n_exp, logits, neg)
    m1 = jnp.max(l1, axis=-1, keepdims=True)
    i1 = jnp.min(jnp.where(l1 == m1, lane, LANE), axis=-1, keepdims=True)
    l2 = jnp.where(lane == i1, neg, l1)
    m2 = jnp.max(l2, axis=-1, keepdims=True)
    i2 = jnp.min(jnp.where(l2 == m2, lane, LANE), axis=-1, keepdims=True)
    g1 = 1.0 / (1.0 + jnp.exp(m2 - m1))
    g2 = 1.0 - g1
    sel1 = lane == i1
    sel2 = lane == i2
    twohot = (sel1 | sel2).astype(F32)
    before = _dot(tri_ref[...], twohot.astype(BF16)) + run_ref[...]
    p1 = jnp.sum(jnp.where(sel1, before, 0.0), axis=-1, keepdims=True)
    p2 = jnp.sum(jnp.where(sel2, before, 0.0), axis=-1, keepdims=True)
    run = run_ref[...] + jnp.sum(twohot, axis=0, keepdims=True)
    run_ref[...] = run
    cnt_ref[...] = jnp.broadcast_to(run, cnt_ref.shape)
    meta = jnp.where(lane == 0, i1.astype(F32), 0.0)
    meta = jnp.where(lane == 1, i2.astype(F32), meta)
    meta = jnp.where(lane == 2, g1, meta)
    meta = jnp.where(lane == 3, g2, meta)
    meta = jnp.where(lane == 4, p1, meta)
    meta = jnp.where(lane == 5, p2, meta)
    meta_ref[...] = meta


def _router(x, norm_w, w_router):
    T, D = x.shape
    n_exp = w_router.shape[1]
    tm = min(512, T)
    wpad = jnp.zeros((D, LANE), F32).at[:, :n_exp].set(w_router.astype(F32))
    whi = wpad.astype(BF16)
    wlo = (wpad - whi.astype(F32)).astype(BF16)
    tri = jnp.asarray(np.tril(np.ones((tm, tm), np.float32), -1), BF16)
    kern = functools.partial(_router_kernel, n_exp=n_exp)
    meta, cnt = pl.pallas_call(
        kern,
        grid=(T // tm,),
        in_specs=[
            pl.BlockSpec((tm, D), lambda i: (i, 0)),
            pl.BlockSpec((1, D), lambda i: (0, 0)),
            pl.BlockSpec((D, LANE), lambda i: (0, 0)),
            pl.BlockSpec((D, LANE), lambda i: (0, 0)),
            pl.BlockSpec((tm, tm), lambda i: (0, 0)),
        ],
        out_specs=[
            pl.BlockSpec((tm, LANE), lambda i: (i, 0)),
            pl.BlockSpec((8, LANE), lambda i: (0, 0)),
        ],
        out_shape=[
            jax.ShapeDtypeStruct((T, LANE), F32),
            jax.ShapeDtypeStruct((8, LANE), F32),
        ],
        scratch_shapes=[pltpu.VMEM((1, LANE), F32)],
        compiler_params=_cparams(("arbitrary",)),
        name="router",
    )(x, norm_w.reshape(1, D), whi, wlo, tri)
    return meta, cnt[0, :n_exp]


def _zero_fill_pads(padlo_ref, padlen_ref, xs_ref, z_scr, sem, *, n_exp, blk, n_blocks):
    z_scr[...] = jnp.zeros_like(z_scr)
    sub = 8
    bits = [1 << k for k in reversed(range(3, int(math.log2(blk))))]

    def pad_copies(run):
        for e in range(n_exp):
            lo = padlo_ref[e]
            ln = padlen_ref[e]
            head = (sub - lo % sub) % sub
            for r in range(sub - 1):
                @pl.when(r < head)
                def _(r=r):
                    run(pltpu.make_async_copy(z_scr.at[pl.ds(0, 1)], xs_ref.at[pl.ds(lo + r, 1)], sem))

            off = lo + head
            rem = ln - head
            for sz in bits:
                take = (rem & sz) != 0

                @pl.when(take)
                def _(off=off, sz=sz):
                    dst = xs_ref.at[pl.ds(pl.multiple_of(off, sub), sz)]
                    run(pltpu.make_async_copy(z_scr.at[pl.ds(0, sz)], dst, sem))

                off = off + jnp.where(take, sz, 0)

        def tail(b, c):
            @pl.when(b * blk >= padlo_ref[n_exp])
            def _():
                dst = xs_ref.at[pl.ds(pl.multiple_of(b * blk, blk), blk)]
                run(pltpu.make_async_copy(z_scr, dst, sem))
            return c

        lax.fori_loop(0, n_blocks, tail, 0)

    pad_copies(lambda cp: cp.start())
    pad_copies(lambda cp: cp.wait())


def _dispatch_kernel(padlo_ref, padlen_ref, d1_ref, d2_ref, x_ref, nw_ref, xs_ref, h_scr, z_scr,
                     sem, zsem, *, tm, n_exp, blk, n_blocks):
    @pl.when(pl.program_id(0) == 0)
    def _():
        _zero_fill_pads(padlo_ref, padlen_ref, xs_ref, z_scr, zsem,
                        n_exp=n_exp, blk=blk, n_blocks=n_blocks)

    h_scr[...] = _rmsnorm(x_ref[...], nw_ref[...])

    def copy(t, dst):
        return pltpu.make_async_copy(h_scr.at[pl.ds(t, 1)], xs_ref.at[pl.ds(dst, 1)], sem)

    def start(t, c):
        copy(t, d1_ref[0, 0, t]).start()
        copy(t, d2_ref[0, 0, t]).start()
        return c

    lax.fori_loop(0, tm, start, 0, unroll=8)
    for _ in range(TOP_K):
        pltpu.make_async_copy(h_scr, xs_ref.at[pl.ds(0, tm)], sem).wait()


def _dispatch(x, norm_w, dest1, dest2, pad_lo, pad_len, n_rows, blk):
    T, D = x.shape
    tm = min(256, T)
    nb = T // tm
    n_exp = pad_len.shape[0]
    kern = functools.partial(_dispatch_kernel, tm=tm, n_exp=n_exp, blk=blk, n_blocks=n_rows // blk)
    smem_spec = pl.BlockSpec((1, 1, tm), lambda i, lo, ln: (i, 0, 0), memory_space=pltpu.SMEM)
    grid_spec = pltpu.PrefetchScalarGridSpec(
        num_scalar_prefetch=2,
        grid=(nb,),
        in_specs=[
            smem_spec,
            smem_spec,
            pl.BlockSpec((tm, D), lambda i, lo, ln: (i, 0)),
            pl.BlockSpec((1, D), lambda i, lo, ln: (0, 0)),
        ],
        out_specs=pl.BlockSpec(memory_space=pl.ANY),
        scratch_shapes=[pltpu.VMEM((tm, D), F32), pltpu.VMEM((blk, D), F32),
                        pltpu.SemaphoreType.DMA(()), pltpu.SemaphoreType.DMA(())],
    )
    return pl.pallas_call(
        kern,
        grid_spec=grid_spec,
        out_shape=jax.ShapeDtypeStruct((n_rows, D), F32),
        compiler_params=_cparams(("arbitrary",)),
        name="dispatch",
    )(pad_lo, pad_len, dest1.reshape(nb, 1, tm), dest2.reshape(nb, 1, tm), x, norm_w.reshape(1, D))


def _expert_kernel(exp_ref, nv_ref, xs_ref, wg_ref, wu_ref, wd_ref, y_ref, xb_scr):
    b = pl.program_id(0)
    nv = nv_ref[b]

    @pl.when(pl.program_id(1) == 0)
    def _():
        xb_scr[...] = xs_ref[...].astype(BF16)
        y_ref[...] = jnp.zeros_like(y_ref)

    @pl.when(nv > 0)
    def _():
        xb = xb_scr[...]
        g = _dot(xb, wg_ref[0])
        u = _dot(xb, wu_ref[0])
        a = (g * _sigmoid(g) * u).astype(BF16)
        y_ref[...] += _dot(a, wd_ref[0])


def _experts(xs, blk_exp, blk_nv, wg, wu, wd):
    n_rows, D = xs.shape
    F = wg.shape[2]
    tm = MOE_BLOCK
    tf = 512 if F % 512 == 0 else F
    grid_spec = pltpu.PrefetchScalarGridSpec(
        num_scalar_prefetch=2,
        grid=(n_rows // tm, F // tf),
        in_specs=[
            pl.BlockSpec((tm, D), lambda b, f, e, n: (b, 0)),
            pl.BlockSpec((1, D, tf), lambda b, f, e, n: (e[b], 0, f)),
            pl.BlockSpec((1, D, tf), lambda b, f, e, n: (e[b], 0, f)),
            pl.BlockSpec((1, tf, D), lambda b, f, e, n: (e[b], f, 0)),
        ],
        out_specs=pl.BlockSpec((tm, D), lambda b, f, e, n: (b, 0)),
        scratch_shapes=[pltpu.VMEM((tm, D), BF16)],
    )
    return pl.pallas_call(
        _expert_kernel,
        grid_spec=grid_spec,
        out_shape=jax.ShapeDtypeStruct((n_rows, D), F32),
        compiler_params=_cparams(("arbitrary", "arbitrary")),
        name="experts",
    )(blk_exp, blk_nv, xs, wg, wu, wd)


def _combine_kernel(d1_ref, d2_ref, x_ref, meta_ref, nw_ref, y_ref, o_ref, ybuf, sem, *, tm, final_norm):
    def copy(t, k, src):
        return pltpu.make_async_copy(y_ref.at[pl.ds(src, 1)], ybuf.at[k, pl.ds(t, 1)], sem)

    def start(t, c):
        copy(t, 0, d1_ref[0, 0, t]).start()
        copy(t, 1, d2_ref[0, 0, t]).start()
        return c

    lax.fori_loop(0, tm, start, 0, unroll=8)
    for k in range(TOP_K):
        pltpu.make_async_copy(y_ref.at[pl.ds(0, tm)], ybuf.at[k], sem).wait()
    meta = meta_ref[...]
    g1 = meta[:, 2:3]
    g2 = meta[:, 3:4]
    out = x_ref[...] + (g1 * ybuf[0] + g2 * ybuf[1])
    if final_norm:
        out = _rmsnorm(out, nw_ref[...])
    o_ref[...] = out


def _combine(x, y, meta, dest1, dest2, final_w):
    T, D = x.shape
    tm = min(256, T)
    nb = T // tm
    final_norm = final_w is not None
    nw = (final_w if final_norm else jnp.ones((D,), F32)).reshape(1, D)
    kern = functools.partial(_combine_kernel, tm=tm, final_norm=final_norm)
    smem_spec = pl.BlockSpec((1, 1, tm), lambda i: (i, 0, 0), memory_space=pltpu.SMEM)
    return pl.pallas_call(
        kern,
        grid=(nb,),
        in_specs=[
            smem_spec,
            smem_spec,
            pl.BlockSpec((tm, D), lambda i: (i, 0)),
            pl.BlockSpec((tm, LANE), lambda i: (i, 0)),
            pl.BlockSpec((1, D), lambda i: (0, 0)),
            pl.BlockSpec(memory_space=pl.ANY),
        ],
        out_specs=pl.BlockSpec((tm, D), lambda i: (i, 0)),
        out_shape=jax.ShapeDtypeStruct((T, D), F32),
        scratch_shapes=[pltpu.VMEM((2, tm, D), F32), pltpu.SemaphoreType.DMA(())],
        compiler_params=_cparams(("arbitrary",)),
        name="combine",
    )(dest1.reshape(nb, 1, tm), dest2.reshape(nb, 1, tm), x, meta, nw, y)


def _moe(x, norm_w, w_router, wg, wu, wd, final_w):
    T, D = x.shape
    n_exp = w_router.shape[1]
    blk = MOE_BLOCK
    meta, counts_f = _router(x, norm_w, w_router)
    counts = counts_f.astype(jnp.int32)
    padded = (counts + blk - 1) // blk * blk
    pad_ends = jnp.cumsum(padded)
    pad_starts = pad_ends - padded
    n_rows = (T * TOP_K + blk - 1) // blk * blk + n_exp * blk
    n_blocks = n_rows // blk
    e1 = meta[:, 0].astype(jnp.int32)
    e2 = meta[:, 1].astype(jnp.int32)
    dest1 = pad_starts[e1] + meta[:, 4].astype(jnp.int32)
    dest2 = pad_starts[e2] + meta[:, 5].astype(jnp.int32)
    blk_start = jnp.arange(n_blocks, dtype=jnp.int32) * blk
    blk_exp = jnp.minimum(jnp.searchsorted(pad_ends, blk_start, side='right'),
                          n_exp - 1).astype(jnp.int32)
    blk_nv = jnp.clip(pad_starts[blk_exp] + counts[blk_exp] - blk_start, 0, blk).astype(jnp.int32)
    pad_lo = jnp.concatenate([pad_starts + counts, pad_ends[-1:]]).astype(jnp.int32)
    pad_len = (padded - counts).astype(jnp.int32)
    xs = _dispatch(x, norm_w, dest1, dest2, pad_lo, pad_len, n_rows, blk)
    y = _experts(xs, blk_exp, blk_nv, wg, wu, wd)
    return _combine(x, y, meta, dest1, dest2, final_w)


def _final_norm_kernel(x_ref, nw_ref, o_ref):
    o_ref[...] = _rmsnorm(x_ref[...], nw_ref[...])


def _final_norm(x, w):
    T, D = x.shape
    tm = min(1024, T)
    return pl.pallas_call(
        _final_norm_kernel,
        grid=(T // tm,),
        in_specs=[pl.BlockSpec((tm, D), lambda i: (i, 0)), pl.BlockSpec((1, D), lambda i: (0, 0))],
        out_specs=pl.BlockSpec((tm, D), lambda i: (i, 0)),
        out_shape=jax.ShapeDtypeStruct((T, D), F32),
        compiler_params=_cparams(("arbitrary",)),
        name="final_norm",
    )(x, w.reshape(1, D))


def kernel(x, attn_norm_w, w_in, hgrn_lb_logits, hgrn_norm_w, s5_a_re, s5_a_im, s5_log_dt,
           s5_b_re, s5_b_im, s5_c_re, s5_c_im, s5_d, s5_w_glu, w_branch_a, w_branch_b,
           w_out, ffn_norm_w, dense_w_gate, dense_w_up, dense_w_down, moe_w_router,
           moe_w_gate, moe_w_up, moe_w_down, final_norm_w):
    bsz, seq, D = x.shape
    depth = w_in.shape[0]
    dh = hgrn_lb_logits.shape[1]
    T = bsz * seq
    lower = jnp.cumsum(jax.nn.softmax(hgrn_lb_logits.astype(F32), axis=0), axis=0)
    lower = lower - lower[0]
    nb = seq // S5_L
    nsteps = max(1, int(math.log2(min(S5_ROWS, nb))))
    xf = x.reshape(T, D).astype(F32)
    for layer in range(depth):
        w = w_in[layer]
        seg = lambda k, n=1: w[:, k * dh:(k + n) * dh]
        w_r = jnp.concatenate([seg(0), seg(2), seg(3), seg(5, 2), seg(7, 2), seg(1), seg(4)],
                              axis=1).astype(BF16)
        qig, gates, f_pre, u_q = _norm_inproj(xf, attn_norm_w[layer], w_r, dh)
        ya = _hgrn2(qig, f_pre, lower[layer], hgrn_norm_w[layer], bsz, seq)
        tables = _s5_tables(s5_a_re[layer], s5_a_im[layer], s5_log_dt[layer], s5_b_re[layer],
                            s5_b_im[layer], s5_c_re[layer], s5_c_im[layer], s5_d[layer],
                            S5_L, nsteps)
        z_q = _s5(u_q, tables, bsz, seq)
        merged = _merge(ya, z_q, gates, s5_w_glu[layer].astype(BF16),
                        w_branch_a[layer].astype(BF16), w_branch_b[layer].astype(BF16))
        xf = _outproj(xf, merged, w_out[layer].astype(BF16))
        last = layer == depth - 1
        j = layer // 2
        if layer % 2 == 0:
            xf = _dense_ffn(xf, ffn_norm_w[layer], dense_w_gate[j].astype(BF16),
                            dense_w_up[j].astype(BF16), dense_w_down[j].astype(BF16))
            if last:
                xf = _final_norm(xf, final_norm_w)
        else:
            xf = _moe(xf, ffn_norm_w[layer], moe_w_router[j], moe_w_gate[j].astype(BF16),
                      moe_w_up[j].astype(BF16), moe_w_down[j].astype(BF16),
                      final_norm_w if last else None)
    return xf.reshape(bsz, seq, D).astype(x.dtype)
```

```python
import functools
import math

import numpy as np
import jax
import jax.numpy as jnp
from jax import lax
from jax.experimental import pallas as pl
from jax.experimental.pallas import tpu as pltpu

F32 = jnp.float32
BF16 = jnp.bfloat16

RMS_EPS = 1e-6
HGRN_HEAD_DIM = 128
S5_GROUP = 16
S5_STATE = 64
S5_MIN_NEG = 1e-4
TOP_K = 2

LANE = 128
VMEM_LIMIT = 58 * 1024 * 1024

HGRN_CHUNK = 128
S5_L = 16
S5_ROWS = 512
MOE_BLOCK = 1024
GROUPS_PER_TILE = LANE // S5_GROUP
STATE_PER_TILE = GROUPS_PER_TILE * S5_STATE


def _cparams(sem):
    return pltpu.CompilerParams(dimension_semantics=sem, vmem_limit_bytes=VMEM_LIMIT)


def _dot(a, b):
    return jnp.dot(a, b, preferred_element_type=F32)


def _dot_nt(a, b):
    return lax.dot_general(a, b, (((1,), (1,)), ((), ())), preferred_element_type=F32)


def _dot_tn(a, b):
    return lax.dot_general(a, b, (((0,), (0,)), ((), ())), preferred_element_type=F32)


def _sigmoid(x):
    return 1.0 / (1.0 + jnp.exp(-x))


def _rmsnorm(x, w):
    ms = jnp.mean(x * x, axis=-1, keepdims=True)
    return x * lax.rsqrt(ms + RMS_EPS) * w


N_BF16_COLS = 7
N_F32_COLS = 2


def _norm_inproj_kernel(x_ref, nw_ref, w_ref, ob_ref, of_ref, h_scr):
    @pl.when(pl.program_id(1) == 0)
    def _():
        h_scr[...] = _rmsnorm(x_ref[...], nw_ref[...]).astype(BF16)

    acc = _dot(h_scr[...], w_ref[...])
    ob_ref[...] = acc.astype(BF16)
    of_ref[...] = acc


def _norm_inproj(x, norm_w, w_r, dh):
    T, D = x.shape
    tm = min(1024, T)
    ncols = N_BF16_COLS + N_F32_COLS
    return pl.pallas_call(
        _norm_inproj_kernel,
        grid=(T // tm, ncols),
        in_specs=[
            pl.BlockSpec((tm, D), lambda i, j: (i, 0)),
            pl.BlockSpec((1, D), lambda i, j: (0, 0)),
            pl.BlockSpec((D, dh), lambda i, j: (0, j)),
        ],
        out_specs=[
            pl.BlockSpec((tm, dh), lambda i, j: (i, jnp.minimum(j, N_BF16_COLS))),
            pl.BlockSpec((tm, dh), lambda i, j: (i, jnp.maximum(j - N_BF16_COLS + 1, 0))),
        ],
        out_shape=[
            jax.ShapeDtypeStruct((T, (N_BF16_COLS + 1) * dh), BF16),
            jax.ShapeDtypeStruct((T, (N_F32_COLS + 1) * dh), F32),
        ],
        scratch_shapes=[pltpu.VMEM((tm, D), BF16)],
        compiler_params=_cparams(("arbitrary", "arbitrary")),
        name="norm_inproj",
    )(x, norm_w.reshape(1, D), w_r)


def _hgrn_tables(C):
    nlev = int(math.log2(C))
    r = np.arange(C)[:, None]
    c = np.arange(C)[None, :]
    mats = [(c <= r), (c > r)]
    masks = [np.eye(C, dtype=bool)]
    for lev in range(1, nlev + 1):
        b = 2 ** lev
        half = b // 2
        mid = (r // b) * b + half
        upper = r >= mid
        mats.append(np.where(upper, (c >= mid) & (c <= r), (c > r) & (c < mid)))
        masks.append(((r // b) == (c // b)) & ((r % b) >= half) & ((c % b) < half))
    w = np.concatenate(mats, axis=0).astype(np.float32)
    wcat = np.concatenate([w, w], axis=1)
    m = np.stack(masks, axis=0).astype(np.float32)
    return wcat, m, nlev


def _hgrn2_kernel(q_ref, i_ref, g_ref, f_ref, lb_ref, nw_ref, wcat_ref, mask_ref,
                  o_ref, st_ref, *, C, nlev, heads):
    hd = HGRN_HEAD_DIM

    @pl.when(pl.program_id(1) == 0)
    def _():
        st_ref[...] = jnp.zeros_like(st_ref)

    x = f_ref[...]
    e = jnp.exp(-jnp.abs(x))
    log_sig = jnp.minimum(x, 0.0) - jnp.log(1.0 + e)
    a_ = lb_ref[0:1, :]
    b_ = lb_ref[1:2, :] + log_sig
    logf = jnp.maximum(a_, b_) + jnp.log(1.0 + jnp.exp(-jnp.abs(a_ - b_)))
    kk = lb_ref[2:3, :] * jnp.where(x >= 0.0, e, 1.0) / (1.0 + e)
    q = q_ref[...].astype(F32)
    qs = q * _sigmoid(q)
    v = i_ref[...]

    hi = logf.astype(BF16)
    lo = (logf - hi.astype(F32)).astype(BF16)
    hl = jnp.concatenate([hi, lo], axis=0)

    def window(idx):
        return jnp.exp(_dot(wcat_ref[idx * C:(idx + 1) * C, :], hl))

    e_q = window(0)
    q_in = (qs * e_q).astype(BF16)
    dec = e_q[C - 1:C, :]
    k_st = (kk * window(1)).astype(BF16)

    scores = [None] * heads
    for lev in range(nlev + 1):
        if lev == 0:
            ql = qs.astype(BF16)
            kl = kk.astype(BF16)
        else:
            e_l = window(1 + lev)
            ql = (qs * e_l).astype(BF16)
            kl = (kk * e_l).astype(BF16)
        m = mask_ref[lev]
        for h in range(heads):
            sl = slice(h * hd, (h + 1) * hd)
            p = _dot_nt(ql[:, sl], kl[:, sl]) * m
            scores[h] = p if scores[h] is None else scores[h] + p

    outs = []
    for h in range(heads):
        sl = slice(h * hd, (h + 1) * hd)
        st = st_ref[h]
        o_h = _dot(scores[h].astype(BF16), v[:, sl]) + _dot_nt(q_in[:, sl], st.astype(BF16))
        st_ref[h] = st * dec[:, sl] + _dot_tn(v[:, sl], k_st[:, sl])
        ms = jnp.mean(o_h * o_h, axis=-1, keepdims=True)
        outs.append(o_h * lax.rsqrt(ms + RMS_EPS))
    o = jnp.concatenate(outs, axis=-1)
    g = g_ref[...].astype(F32)
    o_ref[...] = (o * nw_ref[...] * (g * _sigmoid(g))).astype(BF16)


def _hgrn2(proj_b, proj_f, lb, norm_w, bsz, seq):
    T = proj_f.shape[0]
    dh = lb.shape[0]
    heads = dh // HGRN_HEAD_DIM
    C = min(HGRN_CHUNK, seq)
    wcat_np, mask_np, nlev = _hgrn_tables(C)
    nc = seq // C
    lbp = jnp.stack([jnp.log(lb), jnp.log1p(-lb), 1.0 - lb], axis=0)
    lbp = jnp.concatenate([lbp, jnp.zeros((5, dh), F32)], axis=0)
    nw = jnp.tile(norm_w.astype(F32), heads).reshape(1, dh)
    row = lambda b, c: b * nc + c
    kern = functools.partial(_hgrn2_kernel, C=C, nlev=nlev, heads=heads)
    return pl.pallas_call(
        kern,
        grid=(bsz, nc),
        in_specs=[
            pl.BlockSpec((C, dh), lambda b, c: (row(b, c), 4)),
            pl.BlockSpec((C, dh), lambda b, c: (row(b, c), 5)),
            pl.BlockSpec((C, dh), lambda b, c: (row(b, c), 6)),
            pl.BlockSpec((C, dh), lambda b, c: (row(b, c), 1)),
            pl.BlockSpec((8, dh), lambda b, c: (0, 0)),
            pl.BlockSpec((1, dh), lambda b, c: (0, 0)),
            pl.BlockSpec(wcat_np.shape, lambda b, c: (0, 0)),
            pl.BlockSpec(mask_np.shape, lambda b, c: (0, 0, 0)),
        ],
        out_specs=pl.BlockSpec((C, dh), lambda b, c: (row(b, c), 0)),
        out_shape=jax.ShapeDtypeStruct((T, dh), BF16),
        scratch_shapes=[pltpu.VMEM((heads, HGRN_HEAD_DIM, HGRN_HEAD_DIM), F32)],
        compiler_params=_cparams(("arbitrary", "arbitrary")),
        name="hgrn2",
    )(proj_b, proj_b, proj_b, proj_f, lbp, nw, jnp.asarray(wcat_np, BF16), jnp.asarray(mask_np, F32))


def _s5_tables(a_re, a_im, log_dt, b_re, b_im, c_re, c_im, d_skip, L, nsteps):
    G, P = a_re.shape
    I = S5_GROUP
    gpt = GROUPS_PER_TILE
    nq = G // gpt
    sp = STATE_PER_TILE
    hp = lax.Precision.HIGHEST
    A = lax.complex(jnp.minimum(a_re.astype(F32), -S5_MIN_NEG), a_im.astype(F32))
    dt = jnp.exp(log_dt.astype(F32))[:, None]
    adt = A * dt
    a_bar = jnp.exp(adt)
    B = lax.complex(b_re.astype(F32), b_im.astype(F32))
    b_bar = ((a_bar - 1.0) / A)[..., None] * B
    Cc = lax.complex(c_re.astype(F32), c_im.astype(F32))
    tau = jnp.arange(L + 1, dtype=F32)
    apow = jnp.exp(adt[:, None, :] * tau[None, :, None])
    eye = jnp.eye(gpt, dtype=F32)

    def ri_lanes(z):
        lead = z.shape[:-2]
        return jnp.concatenate([z.real.reshape(*lead, nq, sp), z.imag.reshape(*lead, nq, sp)], axis=-1)

    kt = jnp.einsum('gip,gtp,gpj->gtij', Cc, apow[:, :L], b_bar, precision=hp).real
    kt = kt.reshape(nq, gpt, L, I, I).transpose(0, 2, 1, 4, 3)
    kbd = kt[:, :, :, :, None, :] * eye[None, None, :, None, :, None]
    kbd = kbd.reshape(nq, L, LANE, LANE).transpose(0, 2, 1, 3).reshape(nq, LANE, L * LANE)
    kstrip = jnp.concatenate([jnp.zeros((nq, LANE, (L - 1) * LANE), F32), kbd], axis=-1).astype(BF16)

    bb = b_bar.reshape(nq, gpt, P, I).transpose(0, 1, 3, 2)
    bb = bb[:, :, :, None, :] * eye[None, :, None, :, None]
    bbase = jnp.concatenate([bb.real.reshape(nq, LANE, sp), bb.imag.reshape(nq, LANE, sp)], axis=-1)
    cc = Cc.reshape(nq, gpt, I, P).transpose(0, 1, 3, 2)
    cc = cc[:, :, :, None, :] * eye[None, :, None, :, None]
    cbase = jnp.concatenate([cc.real.reshape(nq, sp, LANE), cc.imag.reshape(nq, sp, LANE)], axis=1)

    aprow = ri_lanes(apow[:, L - 1 - jnp.arange(L)].transpose(1, 0, 2)).transpose(1, 0, 2)
    apc = ri_lanes(apow[:, 1:].transpose(1, 0, 2)).transpose(1, 2, 0)
    apcol = jnp.concatenate([apc, jnp.zeros((nq, 2 * sp, LANE - L), F32)], axis=-1)

    steps = (L * (2.0 ** jnp.arange(nsteps, dtype=F32)))
    alp = ri_lanes(jnp.exp(adt[None] * steps[:, None, None])).transpose(1, 0, 2)

    dflat = jnp.tile(d_skip.astype(F32).reshape(nq, 1, LANE), (1, L, 1)).reshape(nq, 1, L * LANE)
    return kstrip, bbase, cbase, aprow, apcol, alp, dflat


def _s5_expand(k_ref, bb_ref, cb_ref, ar_ref, ac_ref, t_scr, b_scr, c_scr, *, L):
    sp = STATE_PER_TILE
    br = bb_ref[0, :, :sp]
    bi = bb_ref[0, :, sp:]
    cr = cb_ref[0, :sp, :]
    ci = cb_ref[0, sp:, :]
    for t in range(L):
        rows = slice(t * LANE, (t + 1) * LANE)
        off = (L - 1 - t) * LANE
        t_scr[rows, :] = k_ref[0, :, off:off + L * LANE]
        ar = ar_ref[0, t:t + 1, :sp]
        ai = ar_ref[0, t:t + 1, sp:]
        b_scr[rows, :sp] = (br * ar - bi * ai).astype(BF16)
        b_scr[rows, sp:] = (br * ai + bi * ar).astype(BF16)
        acr = ac_ref[0, :sp, t:t + 1]
        aci = ac_ref[0, sp:, t:t + 1]
        c_scr[:sp, rows] = (cr * acr - ci * aci).astype(BF16)
        c_scr[sp:, rows] = (-(cr * aci + ci * acr)).astype(BF16)


def _s5_kernel(u_ref, k_ref, bb_ref, cb_ref, ar_ref, ac_ref, al_ref, d_ref, z_ref,
               carry_ref, t_scr, b_scr, c_scr, *, R, L, nsteps):
    sp = STATE_PER_TILE

    @pl.when((pl.program_id(1) == 0) & (pl.program_id(2) == 0))
    def _():
        _s5_expand(k_ref, bb_ref, cb_ref, ar_ref, ac_ref, t_scr, b_scr, c_scr, L=L)

    @pl.when(pl.program_id(2) == 0)
    def _():
        carry_ref[...] = jnp.zeros_like(carry_ref)

    u32 = jnp.concatenate([u_ref[pl.ds(t, R, stride=L), :] for t in range(L)], axis=-1)
    u = u32.astype(BF16)
    z = _dot(u, b_scr[...])
    re = z[:, :sp]
    im = z[:, sp:]
    row = lax.broadcasted_iota(jnp.int32, (R, sp), 0)
    cre = carry_ref[:, :sp]
    cim = carry_ref[:, sp:]
    a_re = al_ref[0, 0:1, :sp]
    a_im = al_ref[0, 0:1, sp:]
    first = row == 0
    re = re + jnp.where(first, a_re * cre - a_im * cim, 0.0)
    im = im + jnp.where(first, a_re * cim + a_im * cre, 0.0)
    d = 1
    for k in range(nsteps):
        p_re = al_ref[0, k:k + 1, :sp]
        p_im = al_ref[0, k:k + 1, sp:]
        keep = row >= d
        s_re = jnp.where(keep, pltpu.roll(re, d, axis=0), 0.0)
        s_im = jnp.where(keep, pltpu.roll(im, d, axis=0), 0.0)
        re, im = re + p_re * s_re - p_im * s_im, im + p_re * s_im + p_im * s_re
        d *= 2
    h_re = jnp.where(first, cre, pltpu.roll(re, 1, axis=0))
    h_im = jnp.where(first, cim, pltpu.roll(im, 1, axis=0))
    carry_ref[:, :sp] = re[R - 1:R, :]
    carry_ref[:, sp:] = im[R - 1:R, :]
    hprev = jnp.concatenate([h_re, h_im], axis=-1).astype(BF16)
    y = _dot(u, t_scr[...]) + _dot(hprev, c_scr[...]) + d_ref[0] * u32
    zz = 0.5 * y * (1.0 + jnp.tanh(math.sqrt(2.0 / math.pi) * (y + 0.044715 * (y * y * y))))
    for t in range(L):
        z_ref[pl.ds(t, R, stride=L), :] = zz[:, t * LANE:(t + 1) * LANE]


def _s5(proj_f, u_tile0, tables, bsz, seq):
    kstrip, bbase, cbase, aprow, apcol, alp, dflat = tables
    nq = kstrip.shape[0]
    T = proj_f.shape[0]
    L = aprow.shape[1]
    sp2 = 2 * STATE_PER_TILE
    nb = seq // L
    R = min(S5_ROWS, nb)
    nrb = nb // R
    nsteps = alp.shape[1]
    kern = functools.partial(_s5_kernel, R=R, L=L, nsteps=nsteps)
    per_q = lambda shape: pl.BlockSpec((1,) + shape, lambda q, b, r: (q, 0, 0))
    return pl.pallas_call(
        kern,
        grid=(nq, bsz, nrb),
        in_specs=[
            pl.BlockSpec((R * L, LANE), lambda q, b, r: (b * nrb + r, u_tile0 + q)),
            per_q((LANE, (2 * L - 1) * LANE)),
            per_q((LANE, sp2)),
            per_q((sp2, LANE)),
            per_q((L, sp2)),
            per_q((sp2, LANE)),
            per_q((nsteps, sp2)),
            per_q((1, L * LANE)),
        ],
        out_specs=pl.BlockSpec((R * L, LANE), lambda q, b, r: (b * nrb + r, q)),
        out_shape=jax.ShapeDtypeStruct((T, nq * LANE), F32),
        scratch_shapes=[
            pltpu.VMEM((1, sp2), F32),
            pltpu.VMEM((L * LANE, L * LANE), BF16),
            pltpu.VMEM((L * LANE, sp2), BF16),
            pltpu.VMEM((sp2, L * LANE), BF16),
        ],
        compiler_params=_cparams(("arbitrary", "arbitrary", "arbitrary")),
        name="s5",
    )(proj_f, kstrip, bbase, cbase, aprow, apcol, alp, dflat)


def _merge_out_kernel(x_ref, ya_ref, z_ref, ga_ref, gb_ref, wglu_ref, wa_ref, wb_ref, wo_ref, o_ref):
    z = z_ref[...]
    yb = (z * _sigmoid(_dot(z.astype(BF16), wglu_ref[...]))).astype(BF16)
    ga = _sigmoid(ga_ref[...].astype(F32))
    gb = _sigmoid(gb_ref[...].astype(F32))
    m = ga * _dot(ya_ref[...], wa_ref[...]) + gb * _dot(yb, wb_ref[...])
    o_ref[...] = x_ref[...] + _dot(m.astype(BF16), wo_ref[...])


def _merge_out(x, ya, z, proj_b, wglu, wa, wb, wout):
    T, D = x.shape
    dh = ya.shape[1]
    ds5 = z.shape[1]
    tm = min(512, T)
    full = lambda a: pl.BlockSpec(a.shape, lambda i: (0, 0))
    return pl.pallas_call(
        _merge_out_kernel,
        grid=(T // tm,),
        in_specs=[
            pl.BlockSpec((tm, D), lambda i: (i, 0)),
            pl.BlockSpec((tm, dh), lambda i: (i, 0)),
            pl.BlockSpec((tm, ds5), lambda i: (i, 0)),
            pl.BlockSpec((tm, D), lambda i: (i, 0)),
            pl.BlockSpec((tm, D), lambda i: (i, 1)),
            full(wglu), full(wa), full(wb), full(wout),
        ],
        out_specs=pl.BlockSpec((tm, D), lambda i: (i, 0)),
        out_shape=jax.ShapeDtypeStruct((T, D), F32),
        compiler_params=_cparams(("arbitrary",)),
        name="merge_out",
    )(x, ya, z, proj_b, proj_b, wglu, wa, wb, wout)


def _dense_ffn_kernel(x_ref, nw_ref, wg_ref, wu_ref, wd_ref, o_ref, h_scr):
    @pl.when(pl.program_id(1) == 0)
    def _():
        x = x_ref[...]
        h_scr[...] = _rmsnorm(x, nw_ref[...]).astype(BF16)
        o_ref[...] = x

    h = h_scr[...]
    g = _dot(h, wg_ref[...])
    u = _dot(h, wu_ref[...])
    a = (g * _sigmoid(g) * u).astype(BF16)
    o_ref[...] += _dot(a, wd_ref[...])


def _dense_ffn(x, norm_w, wg, wu, wd):
    T, D = x.shape
    F = wg.shape[1]
    tm = min(1024, T)
    tf = 512 if F % 512 == 0 else F
    return pl.pallas_call(
        _dense_ffn_kernel,
        grid=(T // tm, F // tf),
        in_specs=[
            pl.BlockSpec((tm, D), lambda i, f: (i, 0)),
            pl.BlockSpec((1, D), lambda i, f: (0, 0)),
            pl.BlockSpec((D, tf), lambda i, f: (0, f)),
            pl.BlockSpec((D, tf), lambda i, f: (0, f)),
            pl.BlockSpec((tf, D), lambda i, f: (f, 0)),
        ],
        out_specs=pl.BlockSpec((tm, D), lambda i, f: (i, 0)),
        out_shape=jax.ShapeDtypeStruct((T, D), F32),
        scratch_shapes=[pltpu.VMEM((tm, D), BF16)],
        compiler_params=_cparams(("arbitrary", "arbitrary")),
        name="dense_ffn",
    )(x, norm_w.reshape(1, D), wg, wu, wd)


def _router_kernel(x_ref, nw_ref, whi_ref, wlo_ref, tri_ref, meta_ref, cnt_ref, run_ref, *, n_exp):
    i = pl.program_id(0)

    @pl.when(i == 0)
    def _():
        run_ref[...] = jnp.zeros_like(run_ref)

    h = _rmsnorm(x_ref[...], nw_ref[...])
    hi = h.astype(BF16)
    lo = (h - hi.astype(F32)).astype(BF16)
    logits = _dot(hi, whi_ref[...]) + _dot(lo, whi_ref[...]) + _dot(hi, wlo_ref[...])
    tm = logits.shape[0]
    lane = lax.broadcasted_iota(jnp.int32, (tm, LANE), 1)
    neg = jnp.float32(-jnp.inf)
    l1 = jnp.where(lane < n_exp, logits, neg)
    m1 = jnp.max(l1, axis=-1, keepdims=True)
    i1 = jnp.min(jnp.where(l1 == m1, lane, LANE), axis=-1, keepdims=True)
    l2 = jnp.where(lane == i1, neg, l1)
    m2 = jnp.max(l2, axis=-1, keepdims=True)
    i2 = jnp.min(jnp.where(l2 == m2, lane, LANE), axis=-1, keepdims=True)
    g1 = 1.0 / (1.0 + jnp.exp(m2 - m1))
    g2 = 1.0 - g1
    sel1 = lane == i1
    sel2 = lane == i2
    twohot = (sel1 | sel2).astype(F32)
    before = _dot(tri_ref[...], twohot.astype(BF16)) + run_ref[...]
    p1 = jnp.sum(jnp.where(sel1, before, 0.0), axis=-1, keepdims=True)
    p2 = jnp.sum(jnp.where(sel2, before, 0.0), axis=-1, keepdims=True)
    run = run_ref[...] + jnp.sum(twohot, axis=0, keepdims=True)
    run_ref[...] = run
    cnt_ref[...] = jnp.broadcast_to(run, cnt_ref.shape)
    meta = jnp.where(lane == 0, i1.astype(F32), 0.0)
    meta = jnp.where(lane == 1, i2.astype(F32), meta)
    meta = jnp.where(lane == 2, g1, meta)
    meta = jnp.where(lane == 3, g2, meta)
    meta = jnp.where(lane == 4, p1, meta)
    meta = jnp.where(lane == 5, p2, meta)
    meta_ref[...] = meta


def _router(x, norm_w, w_router):
    T, D = x.shape
    n_exp = w_router.shape[1]
    tm = min(512, T)
    wpad = jnp.zeros((D, LANE), F32).at[:, :n_exp].set(w_router.astype(F32))
    whi = wpad.astype(BF16)
    wlo = (wpad - whi.astype(F32)).astype(BF16)
    tri = jnp.asarray(np.tril(np.ones((tm, tm), np.float32), -1), BF16)
    kern = functools.partial(_router_kernel, n_exp=n_exp)
    meta, cnt = pl.pallas_call(
        kern,
        grid=(T // tm,),
        in_specs=[
            pl.BlockSpec((tm, D), lambda i: (i, 0)),
            pl.BlockSpec((1, D), lambda i: (0, 0)),
            pl.BlockSpec((D, LANE), lambda i: (0, 0)),
            pl.BlockSpec((D, LANE), lambda i: (0, 0)),
            pl.BlockSpec((tm, tm), lambda i: (0, 0)),
        ],
        out_specs=[
            pl.BlockSpec((tm, LANE), lambda i: (i, 0)),
            pl.BlockSpec((8, LANE), lambda i: (0, 0)),
        ],
        out_shape=[
            jax.ShapeDtypeStruct((T, LANE), F32),
            jax.ShapeDtypeStruct((8, LANE), F32),
        ],
        scratch_shapes=[pltpu.VMEM((1, LANE), F32)],
        compiler_params=_cparams(("arbitrary",)),
        name="router",
    )(x, norm_w.reshape(1, D), whi, wlo, tri)
    return meta, cnt[0, :n_exp]


def _zero_fill_pads(padlo_ref, padlen_ref, xs_ref, z_scr, sem, *, n_exp, blk, n_blocks):
    z_scr[...] = jnp.zeros_like(z_scr)
    sub = 8
    bits = [1 << k for k in reversed(range(3, int(math.log2(blk))))]

    def pad_copies(run):
        for e in range(n_exp):
            lo = padlo_ref[e]
            ln = padlen_ref[e]
            head = (sub - lo % sub) % sub
            for r in range(sub - 1):
                @pl.when(r < head)
                def _(r=r):
                    run(pltpu.make_async_copy(z_scr.at[pl.ds(0, 1)], xs_ref.at[pl.ds(lo + r, 1)], sem))

            off = lo + head
            rem = ln - head
            for sz in bits:
                take = (rem & sz) != 0

                @pl.when(take)
                def _(off=off, sz=sz):
                    dst = xs_ref.at[pl.ds(pl.multiple_of(off, sub), sz)]
                    run(pltpu.make_async_copy(z_scr.at[pl.ds(0, sz)], dst, sem))

                off = off + jnp.where(take, sz, 0)

        def tail(b, c):
            @pl.when(b * blk >= padlo_ref[n_exp])
            def _():
                dst = xs_ref.at[pl.ds(pl.multiple_of(b * blk, blk), blk)]
                run(pltpu.make_async_copy(z_scr, dst, sem))
            return c

        lax.fori_loop(0, n_blocks, tail, 0)

    pad_copies(lambda cp: cp.start())
    pad_copies(lambda cp: cp.wait())


def _dispatch_kernel(padlo_ref, padlen_ref, d1_ref, d2_ref, x_ref, nw_ref, xs_ref, h_scr, z_scr,
                     sem, zsem, *, tm, n_exp, blk, n_blocks):
    @pl.when(pl.program_id(0) == 0)
    def _():
        _zero_fill_pads(padlo_ref, padlen_ref, xs_ref, z_scr, zsem,
                        n_exp=n_exp, blk=blk, n_blocks=n_blocks)

    i = pl.program_id(0)
    slot = i % 2
    h_scr[slot] = _rmsnorm(x_ref[...], nw_ref[...])

    def start(t, c):
        for d_ref in (d1_ref, d2_ref):
            pltpu.make_async_copy(h_scr.at[slot, pl.ds(t, 1)], xs_ref.at[pl.ds(d_ref[0, 0, t], 1)],
                                  sem.at[slot]).start()
        return c

    lax.fori_loop(0, tm, start, 0, unroll=8)

    def drain(s):
        for _ in range(TOP_K):
            pltpu.make_async_copy(h_scr.at[s], xs_ref.at[pl.ds(0, tm)], sem.at[s]).wait()

    @pl.when(i > 0)
    def _():
        drain(1 - slot)

    @pl.when(i == pl.num_programs(0) - 1)
    def _():
        drain(slot)


def _dispatch(x, norm_w, dest1, dest2, pad_lo, pad_len, n_rows, blk):
    T, D = x.shape
    tm = min(256, T)
    nb = T // tm
    n_exp = pad_len.shape[0]
    kern = functools.partial(_dispatch_kernel, tm=tm, n_exp=n_exp, blk=blk, n_blocks=n_rows // blk)
    smem_spec = pl.BlockSpec((1, 1, tm), lambda i, lo, ln: (i, 0, 0), memory_space=pltpu.SMEM)
    grid_spec = pltpu.PrefetchScalarGridSpec(
        num_scalar_prefetch=2,
        grid=(nb,),
        in_specs=[
            smem_spec,
            smem_spec,
            pl.BlockSpec((tm, D), lambda i, lo, ln: (i, 0)),
            pl.BlockSpec((1, D), lambda i, lo, ln: (0, 0)),
        ],
        out_specs=pl.BlockSpec(memory_space=pl.ANY),
        scratch_shapes=[pltpu.VMEM((2, tm, D), F32), pltpu.VMEM((blk, D), F32),
                        pltpu.SemaphoreType.DMA((2,)), pltpu.SemaphoreType.DMA(())],
    )
    return pl.pallas_call(
        kern,
        grid_spec=grid_spec,
        out_shape=jax.ShapeDtypeStruct((n_rows, D), F32),
        compiler_params=_cparams(("arbitrary",)),
        name="dispatch",
    )(pad_lo, pad_len, dest1.reshape(nb, 1, tm), dest2.reshape(nb, 1, tm), x, norm_w.reshape(1, D))


def _expert_kernel(exp_ref, nv_ref, xs_ref, wg_ref, wu_ref, wd_ref, y_ref, xb_scr):
    b = pl.program_id(0)
    nv = nv_ref[b]

    @pl.when(pl.program_id(1) == 0)
    def _():
        xb_scr[...] = xs_ref[...].astype(BF16)
        y_ref[...] = jnp.zeros_like(y_ref)

    @pl.when(nv > 0)
    def _():
        xb = xb_scr[...]
        g = _dot(xb, wg_ref[0])
        u = _dot(xb, wu_ref[0])
        a = (g * _sigmoid(g) * u).astype(BF16)
        y_ref[...] += _dot(a, wd_ref[0])


def _experts(xs, blk_exp, blk_nv, wg, wu, wd):
    n_rows, D = xs.shape
    F = wg.shape[2]
    tm = MOE_BLOCK
    tf = 512 if F % 512 == 0 else F
    grid_spec = pltpu.PrefetchScalarGridSpec(
        num_scalar_prefetch=2,
        grid=(n_rows // tm, F // tf),
        in_specs=[
            pl.BlockSpec((tm, D), lambda b, f, e, n: (b, 0)),
            pl.BlockSpec((1, D, tf), lambda b, f, e, n: (e[b], 0, f)),
            pl.BlockSpec((1, D, tf), lambda b, f, e, n: (e[b], 0, f)),
            pl.BlockSpec((1, tf, D), lambda b, f, e, n: (e[b], f, 0)),
        ],
        out_specs=pl.BlockSpec((tm, D), lambda b, f, e, n: (b, 0)),
        scratch_shapes=[pltpu.VMEM((tm, D), BF16)],
    )
    return pl.pallas_call(
        _expert_kernel,
        grid_spec=grid_spec,
        out_shape=jax.ShapeDtypeStruct((n_rows, D), F32),
        compiler_params=_cparams(("arbitrary", "arbitrary")),
        name="experts",
    )(blk_exp, blk_nv, xs, wg, wu, wd)


def _combine_kernel(d1_ref, d2_ref, n1_ref, n2_ref, x_ref, meta_ref, nw_ref, y_ref, o_ref, ybuf, sem,
                    *, tm, final_norm):
    i = pl.program_id(0)
    slot = i % 2

    def gather(r1_ref, r2_ref, s):
        def start(t, c):
            for k, r_ref in enumerate((r1_ref, r2_ref)):
                pltpu.make_async_copy(y_ref.at[pl.ds(r_ref[0, 0, t], 1)],
                                      ybuf.at[s, k, pl.ds(t, 1)], sem.at[s]).start()
            return c

        lax.fori_loop(0, tm, start, 0, unroll=8)

    @pl.when(i == 0)
    def _():
        gather(d1_ref, d2_ref, 0)

    @pl.when(i + 1 < pl.num_programs(0))
    def _():
        gather(n1_ref, n2_ref, 1 - slot)

    for k in range(TOP_K):
        pltpu.make_async_copy(y_ref.at[pl.ds(0, tm)], ybuf.at[slot, k], sem.at[slot]).wait()
    meta = meta_ref[...]
    g1 = meta[:, 2:3]
    g2 = meta[:, 3:4]
    out = x_ref[...] + (g1 * ybuf[slot, 0] + g2 * ybuf[slot, 1])
    if final_norm:
        out = _rmsnorm(out, nw_ref[...])
    o_ref[...] = out


def _combine(x, y, meta, dest1, dest2, final_w):
    T, D = x.shape
    tm = min(256, T)
    nb = T // tm
    final_norm = final_w is not None
    nw = (final_w if final_norm else jnp.ones((D,), F32)).reshape(1, D)
    kern = functools.partial(_combine_kernel, tm=tm, final_norm=final_norm)
    cur_spec = pl.BlockSpec((1, 1, tm), lambda i: (i, 0, 0), memory_space=pltpu.SMEM)
    nxt_spec = pl.BlockSpec((1, 1, tm), lambda i: (jnp.minimum(i + 1, nb - 1), 0, 0),
                            memory_space=pltpu.SMEM)
    d1 = dest1.reshape(nb, 1, tm)
    d2 = dest2.reshape(nb, 1, tm)
    return pl.pallas_call(
        kern,
        grid=(nb,),
        in_specs=[
            cur_spec,
            cur_spec,
            nxt_spec,
            nxt_spec,
            pl.BlockSpec((tm, D), lambda i: (i, 0)),
            pl.BlockSpec((tm, LANE), lambda i: (i, 0)),
            pl.BlockSpec((1, D), lambda i: (0, 0)),
            pl.BlockSpec(memory_space=pl.ANY),
        ],
        out_specs=pl.BlockSpec((tm, D), lambda i: (i, 0)),
        out_shape=jax.ShapeDtypeStruct((T, D), F32),
        scratch_shapes=[pltpu.VMEM((2, TOP_K, tm, D), F32), pltpu.SemaphoreType.DMA((2,))],
        compiler_params=_cparams(("arbitrary",)),
        name="combine",
    )(d1, d2, d1, d2, x, meta, nw, y)


def _moe(x, norm_w, w_router, wg, wu, wd, final_w):
    T, D = x.shape
    n_exp = w_router.shape[1]
    blk = MOE_BLOCK
    meta, counts_f = _router(x, norm_w, w_router)
    counts = counts_f.astype(jnp.int32)
    padded = (counts + blk - 1) // blk * blk
    pad_ends = jnp.cumsum(padded)
    pad_starts = pad_ends - padded
    n_rows = (T * TOP_K + blk - 1) // blk * blk + n_exp * blk
    n_blocks = n_rows // blk
    e1 = meta[:, 0].astype(jnp.int32)
    e2 = meta[:, 1].astype(jnp.int32)
    dest1 = pad_starts[e1] + meta[:, 4].astype(jnp.int32)
    dest2 = pad_starts[e2] + meta[:, 5].astype(jnp.int32)
    blk_start = jnp.arange(n_blocks, dtype=jnp.int32) * blk
    blk_exp = jnp.minimum(jnp.searchsorted(pad_ends, blk_start, side='right'),
                          n_exp - 1).astype(jnp.int32)
    blk_nv = jnp.clip(pad_starts[blk_exp] + counts[blk_exp] - blk_start, 0, blk).astype(jnp.int32)
    pad_lo = jnp.concatenate([pad_starts + counts, pad_ends[-1:]]).astype(jnp.int32)
    pad_len = (padded - counts).astype(jnp.int32)
    xs = _dispatch(x, norm_w, dest1, dest2, pad_lo, pad_len, n_rows, blk)
    y = _experts(xs, blk_exp, blk_nv, wg, wu, wd)
    return _combine(x, y, meta, dest1, dest2, final_w)


def _final_norm_kernel(x_ref, nw_ref, o_ref):
    o_ref[...] = _rmsnorm(x_ref[...], nw_ref[...])


def _final_norm(x, w):
    T, D = x.shape
    tm = min(1024, T)
    return pl.pallas_call(
        _final_norm_kernel,
        grid=(T // tm,),
        in_specs=[pl.BlockSpec((tm, D), lambda i: (i, 0)), pl.BlockSpec((1, D), lambda i: (0, 0))],
        out_specs=pl.BlockSpec((tm, D), lambda i: (i, 0)),
        out_shape=jax.ShapeDtypeStruct((T, D), F32),
        compiler_params=_cparams(("arbitrary",)),
        name="final_norm",
    )(x, w.reshape(1, D))


def kernel(x, attn_norm_w, w_in, hgrn_lb_logits, hgrn_norm_w, s5_a_re, s5_a_im, s5_log_dt,
           s5_b_re, s5_b_im, s5_c_re, s5_c_im, s5_d, s5_w_glu, w_branch_a, w_branch_b,
           w_out, ffn_norm_w, dense_w_gate, dense_w_up, dense_w_down, moe_w_router,
           moe_w_gate, moe_w_up, moe_w_down, final_norm_w):
    bsz, seq, D = x.shape
    depth = w_in.shape[0]
    dh = hgrn_lb_logits.shape[1]
    T = bsz * seq
    lower = jnp.cumsum(jax.nn.softmax(hgrn_lb_logits.astype(F32), axis=0), axis=0)
    lower = lower - lower[0]
    nb = seq // S5_L
    nsteps = max(1, int(math.log2(min(S5_ROWS, nb))))
    xf = x.reshape(T, D).astype(F32)
    for layer in range(depth):
        w = w_in[layer]
        seg = lambda k, n=1: w[:, k * dh:(k + n) * dh]
        w_r = jnp.concatenate([seg(5, 2), seg(7, 2), seg(0), seg(2), seg(3), seg(1), seg(4)],
                              axis=1).astype(BF16)
        proj_b, proj_f = _norm_inproj(xf, attn_norm_w[layer], w_r, dh)
        ya = _hgrn2(proj_b, proj_f, lower[layer], hgrn_norm_w[layer], bsz, seq)
        tables = _s5_tables(s5_a_re[layer], s5_a_im[layer], s5_log_dt[layer], s5_b_re[layer],
                            s5_b_im[layer], s5_c_re[layer], s5_c_im[layer], s5_d[layer],
                            S5_L, nsteps)
        z = _s5(proj_f, 2 * dh // LANE, tables, bsz, seq)
        xf = _merge_out(xf, ya, z, proj_b, s5_w_glu[layer].astype(BF16),
                        w_branch_a[layer].astype(BF16), w_branch_b[layer].astype(BF16),
                        w_out[layer].astype(BF16))
        last = layer == depth - 1
        j = layer // 2
        if layer % 2 == 0:
            xf = _dense_ffn(xf, ffn_norm_w[layer], dense_w_gate[j].astype(BF16),
                            dense_w_up[j].astype(BF16), dense_w_down[j].astype(BF16))
            if last:
                xf = _final_norm(xf, final_norm_w)
        else:
            xf = _moe(xf, ffn_norm_w[layer], moe_w_router[j], moe_w_gate[j].astype(BF16),
                      moe_w_up[j].astype(BF16), moe_w_down[j].astype(BF16),
                      final_norm_w if last else None)
    return xf.reshape(bsz, seq, D).astype(x.dtype)
```

```python
import functools
import math

import numpy as np
import jax
import jax.numpy as jnp
from jax import lax
from jax.experimental import pallas as pl
from jax.experimental.pallas import tpu as pltpu

F32 = jnp.float32
BF16 = jnp.bfloat16

RMS_EPS = 1e-6
HGRN_HEAD_DIM = 128
S5_GROUP = 16
S5_STATE = 64
S5_MIN_NEG = 1e-4
TOP_K = 2

LANE = 128
VMEM_LIMIT = 58 * 1024 * 1024

HGRN_CHUNK = 128
S5_L = 16
S5_ROWS = 512
MOE_BLOCK = 1024
GROUPS_PER_TILE = LANE // S5_GROUP
STATE_PER_TILE = GROUPS_PER_TILE * S5_STATE


def _cparams(sem):
    return pltpu.CompilerParams(dimension_semantics=sem, vmem_limit_bytes=VMEM_LIMIT)


def _dot(a, b):
    return jnp.dot(a, b, preferred_element_type=F32)


def _dot_nt(a, b):
    return lax.dot_general(a, b, (((1,), (1,)), ((), ())), preferred_element_type=F32)


def _dot_tn(a, b):
    return lax.dot_general(a, b, (((0,), (0,)), ((), ())), preferred_element_type=F32)


def _sigmoid(x):
    return 1.0 / (1.0 + jnp.exp(-x))


def _rmsnorm(x, w):
    ms = jnp.mean(x * x, axis=-1, keepdims=True)
    return x * lax.rsqrt(ms + RMS_EPS) * w


N_BF16_COLS = 7
N_F32_COLS = 2


def _norm_inproj_kernel(x_ref, nw_ref, w_ref, ob_ref, of_ref, h_scr):
    @pl.when(pl.program_id(1) == 0)
    def _():
        h_scr[...] = _rmsnorm(x_ref[...], nw_ref[...]).astype(BF16)

    acc = _dot(h_scr[...], w_ref[...])
    ob_ref[...] = acc.astype(BF16)
    of_ref[...] = acc


def _norm_inproj(x, norm_w, w_r, dh):
    T, D = x.shape
    tm = min(1024, T)
    ncols = N_BF16_COLS + N_F32_COLS
    return pl.pallas_call(
        _norm_inproj_kernel,
        grid=(T // tm, ncols),
        in_specs=[
            pl.BlockSpec((tm, D), lambda i, j: (i, 0)),
            pl.BlockSpec((1, D), lambda i, j: (0, 0)),
            pl.BlockSpec((D, dh), lambda i, j: (0, j)),
        ],
        out_specs=[
            pl.BlockSpec((tm, dh), lambda i, j: (i, jnp.minimum(j, N_BF16_COLS))),
            pl.BlockSpec((tm, dh), lambda i, j: (i, jnp.maximum(j - N_BF16_COLS + 1, 0))),
        ],
        out_shape=[
            jax.ShapeDtypeStruct((T, (N_BF16_COLS + 1) * dh), BF16),
            jax.ShapeDtypeStruct((T, (N_F32_COLS + 1) * dh), F32),
        ],
        scratch_shapes=[pltpu.VMEM((tm, D), BF16)],
        compiler_params=_cparams(("arbitrary", "arbitrary")),
        name="norm_inproj",
    )(x, norm_w.reshape(1, D), w_r)


def _hgrn_tables(C):
    nlev = int(math.log2(C))
    r = np.arange(C)[:, None]
    c = np.arange(C)[None, :]
    tri = (c <= r).astype(np.float32)
    ltri = np.concatenate([tri, tri, tri], axis=1)
    masks = [np.eye(C, dtype=bool)]
    for lev in range(1, nlev + 1):
        b = 2 ** lev
        half = b // 2
        masks.append(((r // b) == (c // b)) & ((r % b) >= half) & ((c % b) < half))
    m = np.stack(masks, axis=0).astype(np.float32)
    rr = np.arange(C)
    rowsel = np.stack([rr % 2 == 1, rr % 4 == 0, rr % 4 >= 2, rr % 4 == 3]).astype(np.float32)
    return ltri, m, rowsel, nlev


LOG2E = 1.4426950408889634


def _hgrn2_chunk(q_pre, v, g_pre, x, lb_ref, nw_ref, ltri_ref, mask_ref, rs_ref, st_ref, p_scr,
                 *, C, nlev, heads):
    hd = HGRN_HEAD_DIM
    dh = x.shape[1]
    e = jnp.exp(-jnp.abs(x))
    log_sig = jnp.minimum(x, 0.0) - jnp.log(1.0 + e)
    a_ = lb_ref[0:1, :]
    b_ = lb_ref[1:2, :] + log_sig
    logf = jnp.maximum(a_, b_) + jnp.log(1.0 + jnp.exp(-jnp.abs(a_ - b_)))
    kk = lb_ref[2:3, :] * jnp.where(x >= 0.0, e, 1.0) / (1.0 + e)
    q = q_pre.astype(F32)
    qs_b = (q * _sigmoid(q)).astype(BF16)
    kk_b = kk.astype(BF16)

    lf2 = logf * LOG2E
    hi = lf2.astype(BF16)
    r1 = lf2 - hi.astype(F32)
    mid = r1.astype(BF16)
    lo = (r1 - mid.astype(F32)).astype(BF16)
    p = _dot(ltri_ref[...], jnp.concatenate([hi, mid, lo], axis=0))
    p_scr[...] = p

    e_q = jnp.exp2(p)
    q_in = qs_b * e_q.astype(BF16)
    dec = e_q[C - 1:C, :]
    k_st = kk_b * jnp.exp2(p_scr[C - 1:C, :] - p).astype(BF16)

    def level_decay(lev):
        if lev == 1:
            return jnp.exp2(lf2 * rs_ref[0])
        if lev == 2:
            win = (pltpu.roll(lf2, C - 1, axis=0) * rs_ref[1] + lf2 * rs_ref[2]
                   + pltpu.roll(lf2, 1, axis=0) * rs_ref[3])
            return jnp.exp2(win)
        b = 2 ** lev
        half = b // 2
        if half % 8:
            refs = [jnp.broadcast_to(p_scr[k * b + half - 1:k * b + half, :], (b, dh))
                    for k in range(C // b)]
            return jnp.exp2(-jnp.abs(p - jnp.concatenate(refs, axis=0)))
        parts = []
        for k in range(C // b):
            ref = jnp.broadcast_to(p_scr[k * b + half - 1:k * b + half, :], (half, dh))
            parts.append(ref - p_scr[k * b:k * b + half, :])
            parts.append(p_scr[k * b + half:(k + 1) * b, :] - ref)
        return jnp.exp2(jnp.concatenate(parts, axis=0))

    scores = [None] * heads
    for lev in range(nlev + 1):
        if lev == 0:
            ql = qs_b
            kl = kk_b
        else:
            e_l = level_decay(lev).astype(BF16)
            ql = qs_b * e_l
            kl = kk_b * e_l
        m = mask_ref[lev]
        for h in range(heads):
            sl = slice(h * hd, (h + 1) * hd)
            part = _dot_nt(ql[:, sl], kl[:, sl]) * m
            scores[h] = part if scores[h] is None else scores[h] + part

    outs = []
    for h in range(heads):
        sl = slice(h * hd, (h + 1) * hd)
        st = st_ref[h]
        o_h = _dot(scores[h].astype(BF16), v[:, sl]) + _dot_nt(q_in[:, sl], st.astype(BF16))
        st_ref[h] = st * dec[:, sl] + _dot_tn(v[:, sl], k_st[:, sl])
        ms = jnp.mean(o_h * o_h, axis=-1, keepdims=True)
        outs.append(o_h * lax.rsqrt(ms + RMS_EPS))
    o = jnp.concatenate(outs, axis=-1)
    g = g_pre.astype(F32)
    return (o * nw_ref[...] * (g * _sigmoid(g))).astype(BF16)


def _hgrn2_kernel(q_ref, i_ref, g_ref, f_ref, lb_ref, nw_ref, ltri_ref, mask_ref, rs_ref,
                  o_ref, st_ref, p_scr, *, C, nlev, heads):
    @pl.when(pl.program_id(0) == 0)
    def _():
        st_ref[...] = jnp.zeros_like(st_ref)

    for b in range(q_ref.shape[0]):
        o_ref[b] = _hgrn2_chunk(q_ref[b], i_ref[b], g_ref[b], f_ref[b], lb_ref, nw_ref, ltri_ref,
                                mask_ref, rs_ref, st_ref.at[b], p_scr.at[b],
                                C=C, nlev=nlev, heads=heads)


def _hgrn2(proj_b, proj_f, lb, norm_w, bsz, seq):
    T = proj_f.shape[0]
    dh = lb.shape[0]
    heads = dh // HGRN_HEAD_DIM
    C = min(HGRN_CHUNK, seq)
    ltri_np, mask_np, rowsel_np, nlev = _hgrn_tables(C)
    rowsel_np = np.ascontiguousarray(np.broadcast_to(rowsel_np[:, :, None], rowsel_np.shape + (dh,)))
    nc = seq // C
    lbp = jnp.stack([jnp.log(lb), jnp.log1p(-lb), 1.0 - lb], axis=0)
    lbp = jnp.concatenate([lbp, jnp.zeros((5, dh), F32)], axis=0)
    nw = jnp.tile(norm_w.astype(F32), heads).reshape(1, dh)
    pb = proj_b.reshape(bsz, seq, proj_b.shape[1])
    pf = proj_f.reshape(bsz, seq, proj_f.shape[1])
    tile = lambda k: pl.BlockSpec((bsz, C, dh), lambda c: (0, c, k))
    kern = functools.partial(_hgrn2_kernel, C=C, nlev=nlev, heads=heads)
    out = pl.pallas_call(
        kern,
        grid=(nc,),
        in_specs=[
            tile(4), tile(5), tile(6), tile(1),
            pl.BlockSpec((8, dh), lambda c: (0, 0)),
            pl.BlockSpec((1, dh), lambda c: (0, 0)),
            pl.BlockSpec(ltri_np.shape, lambda c: (0, 0)),
            pl.BlockSpec(mask_np.shape, lambda c: (0, 0, 0)),
            pl.BlockSpec(rowsel_np.shape, lambda c: (0, 0, 0)),
        ],
        out_specs=tile(0),
        out_shape=jax.ShapeDtypeStruct((bsz, seq, dh), BF16),
        scratch_shapes=[pltpu.VMEM((bsz, heads, HGRN_HEAD_DIM, HGRN_HEAD_DIM), F32),
                        pltpu.VMEM((bsz, C, dh), F32)],
        compiler_params=_cparams(("arbitrary",)),
        name="hgrn2",
    )(pb, pb, pb, pf, lbp, nw, jnp.asarray(ltri_np, BF16), jnp.asarray(mask_np, F32),
      jnp.asarray(rowsel_np, F32))
    return out.reshape(T, dh)


def _s5_tables(a_re, a_im, log_dt, b_re, b_im, c_re, c_im, d_skip, L, nsteps):
    G, P = a_re.shape
    I = S5_GROUP
    gpt = GROUPS_PER_TILE
    nq = G // gpt
    sp = STATE_PER_TILE
    hp = lax.Precision.HIGHEST
    A = lax.complex(jnp.minimum(a_re.astype(F32), -S5_MIN_NEG), a_im.astype(F32))
    dt = jnp.exp(log_dt.astype(F32))[:, None]
    adt = A * dt
    a_bar = jnp.exp(adt)
    B = lax.complex(b_re.astype(F32), b_im.astype(F32))
    b_bar = ((a_bar - 1.0) / A)[..., None] * B
    Cc = lax.complex(c_re.astype(F32), c_im.astype(F32))
    tau = jnp.arange(L + 1, dtype=F32)
    apow = jnp.exp(adt[:, None, :] * tau[None, :, None])
    eye = jnp.eye(gpt, dtype=F32)

    def ri_lanes(z):
        lead = z.shape[:-2]
        return jnp.concatenate([z.real.reshape(*lead, nq, sp), z.imag.reshape(*lead, nq, sp)], axis=-1)

    kt = jnp.einsum('gip,gtp,gpj->gtij', Cc, apow[:, :L], b_bar, precision=hp).real
    kt = kt.reshape(nq, gpt, L, I, I).transpose(0, 2, 1, 4, 3)
    kbd = kt[:, :, :, :, None, :] * eye[None, None, :, None, :, None]
    kbd = kbd.reshape(nq, L, LANE, LANE).transpose(0, 2, 1, 3).reshape(nq, LANE, L * LANE)
    kstrip = jnp.concatenate([jnp.zeros((nq, LANE, (L - 1) * LANE), F32), kbd], axis=-1).astype(BF16)

    bb = b_bar.reshape(nq, gpt, P, I).transpose(0, 1, 3, 2)
    bb = bb[:, :, :, None, :] * eye[None, :, None, :, None]
    bbase = jnp.concatenate([bb.real.reshape(nq, LANE, sp), bb.imag.reshape(nq, LANE, sp)], axis=-1)
    cc = Cc.reshape(nq, gpt, I, P).transpose(0, 1, 3, 2)
    cc = cc[:, :, :, None, :] * eye[None, :, None, :, None]
    cbase = jnp.concatenate([cc.real.reshape(nq, sp, LANE), cc.imag.reshape(nq, sp, LANE)], axis=1)

    aprow = ri_lanes(apow[:, L - 1 - jnp.arange(L)].transpose(1, 0, 2)).transpose(1, 0, 2)
    apc = ri_lanes(apow[:, 1:].transpose(1, 0, 2)).transpose(1, 2, 0)
    apcol = jnp.concatenate([apc, jnp.zeros((nq, 2 * sp, LANE - L), F32)], axis=-1)

    steps = (L * (2.0 ** jnp.arange(nsteps, dtype=F32)))
    alp = ri_lanes(jnp.exp(adt[None] * steps[:, None, None])).transpose(1, 0, 2)

    dflat = jnp.tile(d_skip.astype(F32).reshape(nq, 1, LANE), (1, L, 1)).reshape(nq, 1, L * LANE)
    return kstrip, bbase, cbase, aprow, apcol, alp, dflat


def _s5_expand(k_ref, bb_ref, cb_ref, ar_ref, ac_ref, t_scr, b_scr, c_scr, *, L):
    sp = STATE_PER_TILE
    br = bb_ref[0, :, :sp]
    bi = bb_ref[0, :, sp:]
    cr = cb_ref[0, :sp, :]
    ci = cb_ref[0, sp:, :]
    for t in range(L):
        rows = slice(t * LANE, (t + 1) * LANE)
        off = (L - 1 - t) * LANE
        t_scr[rows, :] = k_ref[0, :, off:off + L * LANE]
        ar = ar_ref[0, t:t + 1, :sp]
        ai = ar_ref[0, t:t + 1, sp:]
        b_scr[rows, :sp] = (br * ar - bi * ai).astype(BF16)
        b_scr[rows, sp:] = (br * ai + bi * ar).astype(BF16)
        acr = ac_ref[0, :sp, t:t + 1]
        aci = ac_ref[0, sp:, t:t + 1]
        c_scr[:sp, rows] = (cr * acr - ci * aci).astype(BF16)
        c_scr[sp:, rows] = (-(cr * aci + ci * acr)).astype(BF16)


def _s5_kernel(u_ref, k_ref, bb_ref, cb_ref, ar_ref, ac_ref, al_ref, d_ref, z_ref,
               carry_ref, t_scr, b_scr, c_scr, *, R, L, nsteps):
    sp = STATE_PER_TILE

    @pl.when((pl.program_id(1) == 0) & (pl.program_id(2) == 0))
    def _():
        _s5_expand(k_ref, bb_ref, cb_ref, ar_ref, ac_ref, t_scr, b_scr, c_scr, L=L)

    @pl.when(pl.program_id(2) == 0)
    def _():
        carry_ref[...] = jnp.zeros_like(carry_ref)

    u32 = jnp.concatenate([u_ref[pl.ds(t, R, stride=L), :] for t in range(L)], axis=-1)
    u = u32.astype(BF16)
    z = _dot(u, b_scr[...])
    re = z[:, :sp]
    im = z[:, sp:]
    row = lax.broadcasted_iota(jnp.int32, (R, sp), 0)
    cre = carry_ref[:, :sp]
    cim = carry_ref[:, sp:]
    a_re = al_ref[0, 0:1, :sp]
    a_im = al_ref[0, 0:1, sp:]
    first = row == 0
    re = re + jnp.where(first, a_re * cre - a_im * cim, 0.0)
    im = im + jnp.where(first, a_re * cim + a_im * cre, 0.0)
    d = 1
    for k in range(nsteps):
        p_re = al_ref[0, k:k + 1, :sp]
        p_im = al_ref[0, k:k + 1, sp:]
        keep = row >= d
        s_re = jnp.where(keep, pltpu.roll(re, d, axis=0), 0.0)
        s_im = jnp.where(keep, pltpu.roll(im, d, axis=0), 0.0)
        re, im = re + p_re * s_re - p_im * s_im, im + p_re * s_im + p_im * s_re
        d *= 2
    h_re = jnp.where(first, cre, pltpu.roll(re, 1, axis=0))
    h_im = jnp.where(first, cim, pltpu.roll(im, 1, axis=0))
    carry_ref[:, :sp] = re[R - 1:R, :]
    carry_ref[:, sp:] = im[R - 1:R, :]
    hprev = jnp.concatenate([h_re, h_im], axis=-1).astype(BF16)
    y = _dot(u, t_scr[...]) + _dot(hprev, c_scr[...]) + d_ref[0] * u32
    zz = 0.5 * y * (1.0 + jnp.tanh(math.sqrt(2.0 / math.pi) * (y + 0.044715 * (y * y * y))))
    for t in range(L):
        z_ref[pl.ds(t, R, stride=L), :] = zz[:, t * LANE:(t + 1) * LANE]


def _s5(proj_f, u_tile0, tables, bsz, seq):
    kstrip, bbase, cbase, aprow, apcol, alp, dflat = tables
    nq = kstrip.shape[0]
    T = proj_f.shape[0]
    L = aprow.shape[1]
    sp2 = 2 * STATE_PER_TILE
    nb = seq // L
    R = min(S5_ROWS, nb)
    nrb = nb // R
    nsteps = alp.shape[1]
    kern = functools.partial(_s5_kernel, R=R, L=L, nsteps=nsteps)
    per_q = lambda shape: pl.BlockSpec((1,) + shape, lambda q, b, r: (q, 0, 0))
    return pl.pallas_call(
        kern,
        grid=(nq, bsz, nrb),
        in_specs=[
            pl.BlockSpec((R * L, LANE), lambda q, b, r: (b * nrb + r, u_tile0 + q)),
            per_q((LANE, (2 * L - 1) * LANE)),
            per_q((LANE, sp2)),
            per_q((sp2, LANE)),
            per_q((L, sp2)),
            per_q((sp2, LANE)),
            per_q((nsteps, sp2)),
            per_q((1, L * LANE)),
        ],
        out_specs=pl.BlockSpec((R * L, LANE), lambda q, b, r: (b * nrb + r, q)),
        out_shape=jax.ShapeDtypeStruct((T, nq * LANE), F32),
        scratch_shapes=[
            pltpu.VMEM((1, sp2), F32),
            pltpu.VMEM((L * LANE, L * LANE), BF16),
            pltpu.VMEM((L * LANE, sp2), BF16),
            pltpu.VMEM((sp2, L * LANE), BF16),
        ],
        compiler_params=_cparams(("arbitrary", "arbitrary", "arbitrary")),
        name="s5",
    )(proj_f, kstrip, bbase, cbase, aprow, apcol, alp, dflat)


def _merge_out_kernel(x_ref, ya_ref, z_ref, ga_ref, gb_ref, wglu_ref, wa_ref, wb_ref, wo_ref, o_ref):
    z = z_ref[...]
    yb = (z * _sigmoid(_dot(z.astype(BF16), wglu_ref[...]))).astype(BF16)
    ga = _sigmoid(ga_ref[...].astype(F32))
    gb = _sigmoid(gb_ref[...].astype(F32))
    m = ga * _dot(ya_ref[...], wa_ref[...]) + gb * _dot(yb, wb_ref[...])
    o_ref[...] = x_ref[...] + _dot(m.astype(BF16), wo_ref[...])


def _merge_out(x, ya, z, proj_b, wglu, wa, wb, wout):
    T, D = x.shape
    dh = ya.shape[1]
    ds5 = z.shape[1]
    tm = min(512, T)
    full = lambda a: pl.BlockSpec(a.shape, lambda i: (0, 0))
    return pl.pallas_call(
        _merge_out_kernel,
        grid=(T // tm,),
        in_specs=[
            pl.BlockSpec((tm, D), lambda i: (i, 0)),
            pl.BlockSpec((tm, dh), lambda i: (i, 0)),
            pl.BlockSpec((tm, ds5), lambda i: (i, 0)),
            pl.BlockSpec((tm, D), lambda i: (i, 0)),
            pl.BlockSpec((tm, D), lambda i: (i, 1)),
            full(wglu), full(wa), full(wb), full(wout),
        ],
        out_specs=pl.BlockSpec((tm, D), lambda i: (i, 0)),
        out_shape=jax.ShapeDtypeStruct((T, D), F32),
        compiler_params=_cparams(("arbitrary",)),
        name="merge_out",
    )(x, ya, z, proj_b, proj_b, wglu, wa, wb, wout)


def _dense_ffn_kernel(x_ref, nw_ref, wg_ref, wu_ref, wd_ref, o_ref, h_scr):
    @pl.when(pl.program_id(1) == 0)
    def _():
        x = x_ref[...]
        h_scr[...] = _rmsnorm(x, nw_ref[...]).astype(BF16)
        o_ref[...] = x

    h = h_scr[...]
    g = _dot(h, wg_ref[...])
    u = _dot(h, wu_ref[...])
    a = (g * _sigmoid(g) * u).astype(BF16)
    o_ref[...] += _dot(a, wd_ref[...])


def _dense_ffn(x, norm_w, wg, wu, wd):
    T, D = x.shape
    F = wg.shape[1]
    tm = min(1024, T)
    tf = 512 if F % 512 == 0 else F
    return pl.pallas_call(
        _dense_ffn_kernel,
        grid=(T // tm, F // tf),
        in_specs=[
            pl.BlockSpec((tm, D), lambda i, f: (i, 0)),
            pl.BlockSpec((1, D), lambda i, f: (0, 0)),
            pl.BlockSpec((D, tf), lambda i, f: (0, f)),
            pl.BlockSpec((D, tf), lambda i, f: (0, f)),
            pl.BlockSpec((tf, D), lambda i, f: (f, 0)),
        ],
        out_specs=pl.BlockSpec((tm, D), lambda i, f: (i, 0)),
        out_shape=jax.ShapeDtypeStruct((T, D), F32),
        scratch_shapes=[pltpu.VMEM((tm, D), BF16)],
        compiler_params=_cparams(("arbitrary", "arbitrary")),
        name="dense_ffn",
    )(x, norm_w.reshape(1, D), wg, wu, wd)


def _router_kernel(x_ref, nw_ref, whi_ref, wlo_ref, tri_ref, meta_ref, cnt_ref, run_ref, *, n_exp):
    i = pl.program_id(0)

    @pl.when(i == 0)
    def _():
        run_ref[...] = jnp.zeros_like(run_ref)

    h = _rmsnorm(x_ref[...], nw_ref[...])
    hi = h.astype(BF16)
    lo = (h - hi.astype(F32)).astype(BF16)
    logits = _dot(hi, whi_ref[...]) + _dot(lo, whi_ref[...]) + _dot(hi, wlo_ref[...])
    tm = logits.shape[0]
    lane = lax.broadcasted_iota(jnp.int32, (tm, LANE), 1)
    neg = jnp.float32(-jnp.inf)
    l1 = jnp.where(lane < n_exp, logits, neg)
    m1 = jnp.max(l1, axis=-1, keepdims=True)
    i1 = jnp.min(jnp.where(l1 == m1, lane, LANE), axis=-1, keepdims=True)
    l2 = jnp.where(lane == i1, neg, l1)
    m2 = jnp.max(l2, axis=-1, keepdims=True)
    i2 = jnp.min(jnp.where(l2 == m2, lane, LANE), axis=-1, keepdims=True)
    g1 = 1.0 / (1.0 + jnp.exp(m2 - m1))
    g2 = 1.0 - g1
    sel1 = lane == i1
    sel2 = lane == i2
    twohot = (sel1 | sel2).astype(F32)
    before = _dot(tri_ref[...], twohot.astype(BF16)) + run_ref[...]
    p1 = jnp.sum(jnp.where(sel1, before, 0.0), axis=-1, keepdims=True)
    p2 = jnp.sum(jnp.where(sel2, before, 0.0), axis=-1, keepdims=True)
    run = run_ref[...] + jnp.sum(twohot, axis=0, keepdims=True)
    run_ref[...] = run
    cnt_ref[...] = jnp.broadcast_to(run, cnt_ref.shape)
    meta = jnp.where(lane == 0, i1.astype(F32), 0.0)
    meta = jnp.where(lane == 1, i2.astype(F32), meta)
    meta = jnp.where(lane == 2, g1, meta)
    meta = jnp.where(lane == 3, g2, meta)
    meta = jnp.where(lane == 4, p1, meta)
    meta = jnp.where(lane == 5, p2, meta)
    meta_ref[...] = meta


def _router(x, norm_w, w_router):
    T, D = x.shape
    n_exp = w_router.shape[1]
    tm = min(512, T)
    wpad = jnp.zeros((D, LANE), F32).at[:, :n_exp].set(w_router.astype(F32))
    whi = wpad.astype(BF16)
    wlo = (wpad - whi.astype(F32)).astype(BF16)
    tri = jnp.asarray(np.tril(np.ones((tm, tm), np.float32), -1), BF16)
    kern = functools.partial(_router_kernel, n_exp=n_exp)
    meta, cnt = pl.pallas_call(
        kern,
        grid=(T // tm,),
        in_specs=[
            pl.BlockSpec((tm, D), lambda i: (i, 0)),
            pl.BlockSpec((1, D), lambda i: (0, 0)),
            pl.BlockSpec((D, LANE), lambda i: (0, 0)),
            pl.BlockSpec((D, LANE), lambda i: (0, 0)),
            pl.BlockSpec((tm, tm), lambda i: (0, 0)),
        ],
        out_specs=[
            pl.BlockSpec((tm, LANE), lambda i: (i, 0)),
            pl.BlockSpec((8, LANE), lambda i: (0, 0)),
        ],
        out_shape=[
            jax.ShapeDtypeStruct((T, LANE), F32),
            jax.ShapeDtypeStruct((8, LANE), F32),
        ],
        scratch_shapes=[pltpu.VMEM((1, LANE), F32)],
        compiler_params=_cparams(("arbitrary",)),
        name="router",
    )(x, norm_w.reshape(1, D), whi, wlo, tri)
    return meta, cnt[0, :n_exp]


def _zero_fill_pads(padlo_ref, padlen_ref, xs_ref, z_scr, sem, *, n_exp, blk, n_blocks):
    z_scr[...] = jnp.zeros_like(z_scr)
    sub = 8
    bits = [1 << k for k in reversed(range(3, int(math.log2(blk))))]

    def pad_copies(run):
        for e in range(n_exp):
            lo = padlo_ref[e]
            ln = padlen_ref[e]
            head = (sub - lo % sub) % sub
            for r in range(sub - 1):
                @pl.when(r < head)
                def _(r=r):
                    run(pltpu.make_async_copy(z_scr.at[pl.ds(0, 1)], xs_ref.at[pl.ds(lo + r, 1)], sem))

            off = lo + head
            rem = ln - head
            for sz in bits:
                take = (rem & sz) != 0

                @pl.when(take)
                def _(off=off, sz=sz):
                    dst = xs_ref.at[pl.ds(pl.multiple_of(off, sub), sz)]
                    run(pltpu.make_async_copy(z_scr.at[pl.ds(0, sz)], dst, sem))

                off = off + jnp.where(take, sz, 0)

        def tail(b, c):
            @pl.when(b * blk >= padlo_ref[n_exp])
            def _():
                dst = xs_ref.at[pl.ds(pl.multiple_of(b * blk, blk), blk)]
                run(pltpu.make_async_copy(z_scr, dst, sem))
            return c

        lax.fori_loop(0, n_blocks, tail, 0)

    pad_copies(lambda cp: cp.start())
    pad_copies(lambda cp: cp.wait())


def _dispatch_kernel(padlo_ref, padlen_ref, d1_ref, d2_ref, x_ref, nw_ref, xs_ref, h_scr, z_scr,
                     sem, zsem, *, tm, n_exp, blk, n_blocks):
    @pl.when(pl.program_id(0) == 0)
    def _():
        _zero_fill_pads(padlo_ref, padlen_ref, xs_ref, z_scr, zsem,
                        n_exp=n_exp, blk=blk, n_blocks=n_blocks)

    i = pl.program_id(0)
    slot = i % 2
    h_scr[slot] = _rmsnorm(x_ref[...], nw_ref[...])

    def start(t, c):
        for d_ref in (d1_ref, d2_ref):
            pltpu.make_async_copy(h_scr.at[slot, pl.ds(t, 1)], xs_ref.at[pl.ds(d_ref[0, 0, t], 1)],
                                  sem.at[slot]).start()
        return c

    lax.fori_loop(0, tm, start, 0, unroll=8)

    def drain(s):
        for _ in range(TOP_K):
            pltpu.make_async_copy(h_scr.at[s], xs_ref.at[pl.ds(0, tm)], sem.at[s]).wait()

    @pl.when(i > 0)
    def _():
        drain(1 - slot)

    @pl.when(i == pl.num_programs(0) - 1)
    def _():
        drain(slot)


def _dispatch(x, norm_w, dest1, dest2, pad_lo, pad_len, n_rows, blk):
    T, D = x.shape
    tm = min(256, T)
    nb = T // tm
    n_exp = pad_len.shape[0]
    kern = functools.partial(_dispatch_kernel, tm=tm, n_exp=n_exp, blk=blk, n_blocks=n_rows // blk)
    smem_spec = pl.BlockSpec((1, 1, tm), lambda i, lo, ln: (i, 0, 0), memory_space=pltpu.SMEM)
    grid_spec = pltpu.PrefetchScalarGridSpec(
        num_scalar_prefetch=2,
        grid=(nb,),
        in_specs=[
            smem_spec,
            smem_spec,
            pl.BlockSpec((tm, D), lambda i, lo, ln: (i, 0)),
            pl.BlockSpec((1, D), lambda i, lo, ln: (0, 0)),
        ],
        out_specs=pl.BlockSpec(memory_space=pl.ANY),
        scratch_shapes=[pltpu.VMEM((2, tm, D), F32), pltpu.VMEM((blk, D), F32),
                        pltpu.SemaphoreType.DMA((2,)), pltpu.SemaphoreType.DMA(())],
    )
    return pl.pallas_call(
        kern,
        grid_spec=grid_spec,
        out_shape=jax.ShapeDtypeStruct((n_rows, D), F32),
        compiler_params=_cparams(("arbitrary",)),
        name="dispatch",
    )(pad_lo, pad_len, dest1.reshape(nb, 1, tm), dest2.reshape(nb, 1, tm), x, norm_w.reshape(1, D))


def _expert_kernel(exp_ref, nv_ref, xs_ref, wg_ref, wu_ref, wd_ref, y_ref, xb_scr):
    b = pl.program_id(0)
    nv = nv_ref[b]

    @pl.when(pl.program_id(1) == 0)
    def _():
        xb_scr[...] = xs_ref[...].astype(BF16)
        y_ref[...] = jnp.zeros_like(y_ref)

    @pl.when(nv > 0)
    def _():
        xb = xb_scr[...]
        g = _dot(xb, wg_ref[0])
        u = _dot(xb, wu_ref[0])
        a = (g * _sigmoid(g) * u).astype(BF16)
        y_ref[...] += _dot(a, wd_ref[0])


def _experts(xs, blk_exp, blk_nv, wg, wu, wd):
    n_rows, D = xs.shape
    F = wg.shape[2]
    tm = MOE_BLOCK
    tf = 512 if F % 512 == 0 else F
    grid_spec = pltpu.PrefetchScalarGridSpec(
        num_scalar_prefetch=2,
        grid=(n_rows // tm, F // tf),
        in_specs=[
            pl.BlockSpec((tm, D), lambda b, f, e, n: (b, 0)),
            pl.BlockSpec((1, D, tf), lambda b, f, e, n: (e[b], 0, f)),
            pl.BlockSpec((1, D, tf), lambda b, f, e, n: (e[b], 0, f)),
            pl.BlockSpec((1, tf, D), lambda b, f, e, n: (e[b], f, 0)),
        ],
        out_specs=pl.BlockSpec((tm, D), lambda b, f, e, n: (b, 0)),
        scratch_shapes=[pltpu.VMEM((tm, D), BF16)],
    )
    return pl.pallas_call(
        _expert_kernel,
        grid_spec=grid_spec,
        out_shape=jax.ShapeDtypeStruct((n_rows, D), F32),
        compiler_params=_cparams(("arbitrary", "arbitrary")),
        name="experts",
    )(blk_exp, blk_nv, xs, wg, wu, wd)


def _combine_kernel(d1_ref, d2_ref, n1_ref, n2_ref, x_ref, meta_ref, nw_ref, y_ref, o_ref, ybuf, sem,
                    *, tm, final_norm):
    i = pl.program_id(0)
    slot = i % 2

    def gather(r1_ref, r2_ref, s):
        def start(t, c):
            for k, r_ref in enumerate((r1_ref, r2_ref)):
                pltpu.make_async_copy(y_ref.at[pl.ds(r_ref[0, 0, t], 1)],
                                      ybuf.at[s, k, pl.ds(t, 1)], sem.at[s]).start()
            return c

        lax.fori_loop(0, tm, start, 0, unroll=8)

    @pl.when(i == 0)
    def _():
        gather(d1_ref, d2_ref, 0)

    @pl.when(i + 1 < pl.num_programs(0))
    def _():
        gather(n1_ref, n2_ref, 1 - slot)

    for k in range(TOP_K):
        pltpu.make_async_copy(y_ref.at[pl.ds(0, tm)], ybuf.at[slot, k], sem.at[slot]).wait()
    meta = meta_ref[...]
    g1 = meta[:, 2:3]
    g2 = meta[:, 3:4]
    out = x_ref[...] + (g1 * ybuf[slot, 0] + g2 * ybuf[slot, 1])
    if final_norm:
        out = _rmsnorm(out, nw_ref[...])
    o_ref[...] = out


def _combine(x, y, meta, dest1, dest2, final_w):
    T, D = x.shape
    tm = min(256, T)
    nb = T // tm
    final_norm = final_w is not None
    nw = (final_w if final_norm else jnp.ones((D,), F32)).reshape(1, D)
    kern = functools.partial(_combine_kernel, tm=tm, final_norm=final_norm)
    cur_spec = pl.BlockSpec((1, 1, tm), lambda i: (i, 0, 0), memory_space=pltpu.SMEM)
    nxt_spec = pl.BlockSpec((1, 1, tm), lambda i: (jnp.minimum(i + 1, nb - 1), 0, 0),
                            memory_space=pltpu.SMEM)
    d1 = dest1.reshape(nb, 1, tm)
    d2 = dest2.reshape(nb, 1, tm)
    return pl.pallas_call(
        kern,
        grid=(nb,),
        in_specs=[
            cur_spec,
            cur_spec,
            nxt_spec,
            nxt_spec,
            pl.BlockSpec((tm, D), lambda i: (i, 0)),
            pl.BlockSpec((tm, LANE), lambda i: (i, 0)),
            pl.BlockSpec((1, D), lambda i: (0, 0)),
            pl.BlockSpec(memory_space=pl.ANY),
        ],
        out_specs=pl.BlockSpec((tm, D), lambda i: (i, 0)),
        out_shape=jax.ShapeDtypeStruct((T, D), F32),
        scratch_shapes=[pltpu.VMEM((2, TOP_K, tm, D), F32), pltpu.SemaphoreType.DMA((2,))],
        compiler_params=_cparams(("arbitrary",)),
        name="combine",
    )(d1, d2, d1, d2, x, meta, nw, y)


def _moe(x, norm_w, w_router, wg, wu, wd, final_w):
    T, D = x.shape
    n_exp = w_router.shape[1]
    blk = MOE_BLOCK
    meta, counts_f = _router(x, norm_w, w_router)
    counts = counts_f.astype(jnp.int32)
    padded = (counts + blk - 1) // blk * blk
    pad_ends = jnp.cumsum(padded)
    pad_starts = pad_ends - padded
    n_rows = (T * TOP_K + blk - 1) // blk * blk + n_exp * blk
    n_blocks = n_rows // blk
    e1 = meta[:, 0].astype(jnp.int32)
    e2 = meta[:, 1].astype(jnp.int32)
    dest1 = pad_starts[e1] + meta[:, 4].astype(jnp.int32)
    dest2 = pad_starts[e2] + meta[:, 5].astype(jnp.int32)
    blk_start = jnp.arange(n_blocks, dtype=jnp.int32) * blk
    blk_exp = jnp.minimum(jnp.searchsorted(pad_ends, blk_start, side='right'),
                          n_exp - 1).astype(jnp.int32)
    blk_nv = jnp.clip(pad_starts[blk_exp] + counts[blk_exp] - blk_start, 0, blk).astype(jnp.int32)
    pad_lo = jnp.concatenate([pad_starts + counts, pad_ends[-1:]]).astype(jnp.int32)
    pad_len = (padded - counts).astype(jnp.int32)
    xs = _dispatch(x, norm_w, dest1, dest2, pad_lo, pad_len, n_rows, blk)
    y = _experts(xs, blk_exp, blk_nv, wg, wu, wd)
    return _combine(x, y, meta, dest1, dest2, final_w)


def _final_norm_kernel(x_ref, nw_ref, o_ref):
    o_ref[...] = _rmsnorm(x_ref[...], nw_ref[...])


def _final_norm(x, w):
    T, D = x.shape
    tm = min(1024, T)
    return pl.pallas_call(
        _final_norm_kernel,
        grid=(T // tm,),
        in_specs=[pl.BlockSpec((tm, D), lambda i: (i, 0)), pl.BlockSpec((1, D), lambda i: (0, 0))],
        out_specs=pl.BlockSpec((tm, D), lambda i: (i, 0)),
        out_shape=jax.ShapeDtypeStruct((T, D), F32),
        compiler_params=_cparams(("arbitrary",)),
        name="final_norm",
    )(x, w.reshape(1, D))


def kernel(x, attn_norm_w, w_in, hgrn_lb_logits, hgrn_norm_w, s5_a_re, s5_a_im, s5_log_dt,
           s5_b_re, s5_b_im, s5_c_re, s5_c_im, s5_d, s5_w_glu, w_branch_a, w_branch_b,
           w_out, ffn_norm_w, dense_w_gate, dense_w_up, dense_w_down, moe_w_router,
           moe_w_gate, moe_w_up, moe_w_down, final_norm_w):
    bsz, seq, D = x.shape
    depth = w_in.shape[0]
    dh = hgrn_lb_logits.shape[1]
    T = bsz * seq
    lower = jnp.cumsum(jax.nn.softmax(hgrn_lb_logits.astype(F32), axis=0), axis=0)
    lower = lower - lower[0]
    nb = seq // S5_L
    nsteps = max(1, int(math.log2(min(S5_ROWS, nb))))
    xf = x.reshape(T, D).astype(F32)
    for layer in range(depth):
        w = w_in[layer]
        seg = lambda k, n=1: w[:, k * dh:(k + n) * dh]
        w_r = jnp.concatenate([seg(5, 2), seg(7, 2), seg(0), seg(2), seg(3), seg(1), seg(4)],
                              axis=1).astype(BF16)
        proj_b, proj_f = _norm_inproj(xf, attn_norm_w[layer], w_r, dh)
        ya = _hgrn2(proj_b, proj_f, lower[layer], hgrn_norm_w[layer], bsz, seq)
        tables = _s5_tables(s5_a_re[layer], s5_a_im[layer], s5_log_dt[layer], s5_b_re[layer],
                            s5_b_im[layer], s5_c_re[layer], s5_c_im[layer], s5_d[layer],
                            S5_L, nsteps)
        z = _s5(proj_f, 2 * dh // LANE, tables, bsz, seq)
        xf = _merge_out(xf, ya, z, proj_b, s5_w_glu[layer].astype(BF16),
                        w_branch_a[layer].astype(BF16), w_branch_b[layer].astype(BF16),
                        w_out[layer].astype(BF16))
        last = layer == depth - 1
        j = layer // 2
        if layer % 2 == 0:
            xf = _dense_ffn(xf, ffn_norm_w[layer], dense_w_gate[j].astype(BF16),
                            dense_w_up[j].astype(BF16), dense_w_down[j].astype(BF16))
            if last:
                xf = _final_norm(xf, final_norm_w)
        else:
            xf = _moe(xf, ffn_norm_w[layer], moe_w_router[j], moe_w_gate[j].astype(BF16),
                      moe_w_up[j].astype(BF16), moe_w_down[j].astype(BF16),
                      final_norm_w if last else None)
    return xf.reshape(bsz, seq, D).astype(x.dtype)
```

```python
import functools
import math

import numpy as np
import jax
import jax.numpy as jnp
from jax import lax
from jax.experimental import pallas as pl
from jax.experimental.pallas import tpu as pltpu

F32 = jnp.float32
BF16 = jnp.bfloat16

RMS_EPS = 1e-6
HGRN_HEAD_DIM = 128
S5_GROUP = 16
S5_STATE = 64
S5_MIN_NEG = 1e-4
TOP_K = 2

LANE = 128
SUBLANES = 8
VMEM_LIMIT = 58 * 1024 * 1024

HGRN_CHUNK = 128
S5_L = 16
S5_ROWS = 512
MOE_BLOCK = 1024
GROUPS_PER_TILE = LANE // S5_GROUP
STATE_PER_TILE = GROUPS_PER_TILE * S5_STATE


def _cparams(sem):
    return pltpu.CompilerParams(dimension_semantics=sem, vmem_limit_bytes=VMEM_LIMIT)


def _dot(a, b):
    return jnp.dot(a, b, preferred_element_type=F32)


def _dot_nt(a, b):
    return lax.dot_general(a, b, (((1,), (1,)), ((), ())), preferred_element_type=F32)


def _dot_tn(a, b):
    return lax.dot_general(a, b, (((0,), (0,)), ((), ())), preferred_element_type=F32)


def _sigmoid(x):
    return 1.0 / (1.0 + jnp.exp(-x))


def _rmsnorm(x, w):
    ms = jnp.mean(x * x, axis=-1, keepdims=True)
    return x * lax.rsqrt(ms + RMS_EPS) * w


N_BF16_COLS = 7
N_F32_COLS = 2


def _norm_inproj_kernel(x_ref, nw_ref, w_ref, ob_ref, of_ref, h_scr):
    @pl.when(pl.program_id(1) == 0)
    def _():
        h_scr[...] = _rmsnorm(x_ref[...], nw_ref[...]).astype(BF16)

    acc = _dot(h_scr[...], w_ref[...])
    ob_ref[...] = acc.astype(BF16)
    of_ref[...] = acc


def _norm_inproj(x, norm_w, w_r, dh):
    T, D = x.shape
    tm = min(1024, T)
    ncols = N_BF16_COLS + N_F32_COLS
    return pl.pallas_call(
        _norm_inproj_kernel,
        grid=(T // tm, ncols),
        in_specs=[
            pl.BlockSpec((tm, D), lambda i, j: (i, 0)),
            pl.BlockSpec((1, D), lambda i, j: (0, 0)),
            pl.BlockSpec((D, dh), lambda i, j: (0, j)),
        ],
        out_specs=[
            pl.BlockSpec((tm, dh), lambda i, j: (i, jnp.minimum(j, N_BF16_COLS))),
            pl.BlockSpec((tm, dh), lambda i, j: (i, jnp.maximum(j - N_BF16_COLS + 1, 0))),
        ],
        out_shape=[
            jax.ShapeDtypeStruct((T, (N_BF16_COLS + 1) * dh), BF16),
            jax.ShapeDtypeStruct((T, (N_F32_COLS + 1) * dh), F32),
        ],
        scratch_shapes=[pltpu.VMEM((tm, D), BF16)],
        compiler_params=_cparams(("arbitrary", "arbitrary")),
        name="norm_inproj",
    )(x, norm_w.reshape(1, D), w_r)


def _hgrn_tables(C):
    nlev = int(math.log2(C))
    r = np.arange(C)[:, None]
    c = np.arange(C)[None, :]
    tri = (c <= r).astype(np.float32)
    ltri = np.concatenate([tri, tri, tri], axis=1)
    masks = [np.eye(C, dtype=bool)]
    for lev in range(1, nlev + 1):
        b = 2 ** lev
        half = b // 2
        masks.append(((r // b) == (c // b)) & ((r % b) >= half) & ((c % b) < half))
    m = np.stack(masks, axis=0).astype(np.float32)
    rr = np.arange(C)
    rowsel = np.stack([rr % 2 == 1, rr % 4 == 0, rr % 4 >= 2, rr % 4 == 3]).astype(np.float32)
    return ltri, m, rowsel, nlev


LOG2E = 1.4426950408889634


def _hgrn2_chunk(q_pre, v, g_pre, x, lb_ref, nw_ref, ltri_ref, mask_ref, rs_ref, st_ref, p_scr,
                 *, C, nlev, heads):
    hd = HGRN_HEAD_DIM
    dh = x.shape[1]
    e = jnp.exp(-jnp.abs(x))
    log_sig = jnp.minimum(x, 0.0) - jnp.log(1.0 + e)
    a_ = lb_ref[0:1, :]
    b_ = lb_ref[1:2, :] + log_sig
    logf = jnp.maximum(a_, b_) + jnp.log(1.0 + jnp.exp(-jnp.abs(a_ - b_)))
    kk = lb_ref[2:3, :] * jnp.where(x >= 0.0, e, 1.0) / (1.0 + e)
    q = q_pre.astype(F32)
    qs_b = (q * _sigmoid(q)).astype(BF16)
    kk_b = kk.astype(BF16)

    lf2 = logf * LOG2E
    hi = lf2.astype(BF16)
    r1 = lf2 - hi.astype(F32)
    mid = r1.astype(BF16)
    lo = (r1 - mid.astype(F32)).astype(BF16)
    p = _dot(ltri_ref[...], jnp.concatenate([hi, mid, lo], axis=0))
    p_scr[...] = p

    e_q = jnp.exp2(p)
    q_in = qs_b * e_q.astype(BF16)
    dec = e_q[C - 1:C, :]
    k_st = kk_b * jnp.exp2(p_scr[C - 1:C, :] - p).astype(BF16)

    def level_decay(lev):
        if lev == 1:
            return jnp.exp2(lf2 * rs_ref[0])
        if lev == 2:
            win = (pltpu.roll(lf2, C - 1, axis=0) * rs_ref[1] + lf2 * rs_ref[2]
                   + pltpu.roll(lf2, 1, axis=0) * rs_ref[3])
            return jnp.exp2(win)
        b = 2 ** lev
        half = b // 2
        if half % 8:
            refs = [jnp.broadcast_to(p_scr[k * b + half - 1:k * b + half, :], (b, dh))
                    for k in range(C // b)]
            return jnp.exp2(-jnp.abs(p - jnp.concatenate(refs, axis=0)))
        parts = []
        for k in range(C // b):
            ref = jnp.broadcast_to(p_scr[k * b + half - 1:k * b + half, :], (half, dh))
            parts.append(ref - p_scr[k * b:k * b + half, :])
            parts.append(p_scr[k * b + half:(k + 1) * b, :] - ref)
        return jnp.exp2(jnp.concatenate(parts, axis=0))

    scores = [None] * heads
    for lev in range(nlev + 1):
        if lev == 0:
            ql = qs_b
            kl = kk_b
        else:
            e_l = level_decay(lev).astype(BF16)
            ql = qs_b * e_l
            kl = kk_b * e_l
        m = mask_ref[lev]
        for h in range(heads):
            sl = slice(h * hd, (h + 1) * hd)
            part = _dot_nt(ql[:, sl], kl[:, sl]) * m
            scores[h] = part if scores[h] is None else scores[h] + part

    outs = []
    for h in range(heads):
        sl = slice(h * hd, (h + 1) * hd)
        st = st_ref[h]
        o_h = _dot(scores[h].astype(BF16), v[:, sl]) + _dot_nt(q_in[:, sl], st.astype(BF16))
        st_ref[h] = st * dec[:, sl] + _dot_tn(v[:, sl], k_st[:, sl])
        ms = jnp.mean(o_h * o_h, axis=-1, keepdims=True)
        outs.append(o_h * lax.rsqrt(ms + RMS_EPS))
    o = jnp.concatenate(outs, axis=-1)
    g = g_pre.astype(F32)
    return (o * nw_ref[...] * (g * _sigmoid(g))).astype(BF16)


def _hgrn2_kernel(q_ref, i_ref, g_ref, f_ref, lb_ref, nw_ref, ltri_ref, mask_ref, rs_ref,
                  o_ref, st_ref, p_scr, *, C, nlev, heads):
    @pl.when(pl.program_id(0) == 0)
    def _():
        st_ref[...] = jnp.zeros_like(st_ref)

    for b in range(q_ref.shape[0]):
        o_ref[b] = _hgrn2_chunk(q_ref[b], i_ref[b], g_ref[b], f_ref[b], lb_ref, nw_ref, ltri_ref,
                                mask_ref, rs_ref, st_ref.at[b], p_scr.at[b],
                                C=C, nlev=nlev, heads=heads)


def _hgrn2(proj_b, proj_f, lb, norm_w, bsz, seq):
    T = proj_f.shape[0]
    dh = lb.shape[0]
    heads = dh // HGRN_HEAD_DIM
    C = min(HGRN_CHUNK, seq)
    ltri_np, mask_np, rowsel_np, nlev = _hgrn_tables(C)
    rowsel_np = np.ascontiguousarray(np.broadcast_to(rowsel_np[:, :, None], rowsel_np.shape + (dh,)))
    nc = seq // C
    lbp = jnp.stack([jnp.log(lb), jnp.log1p(-lb), 1.0 - lb], axis=0)
    lbp = jnp.concatenate([lbp, jnp.zeros((5, dh), F32)], axis=0)
    nw = jnp.tile(norm_w.astype(F32), heads).reshape(1, dh)
    pb = proj_b.reshape(bsz, seq, proj_b.shape[1])
    pf = proj_f.reshape(bsz, seq, proj_f.shape[1])
    tile = lambda k: pl.BlockSpec((bsz, C, dh), lambda c: (0, c, k))
    kern = functools.partial(_hgrn2_kernel, C=C, nlev=nlev, heads=heads)
    out = pl.pallas_call(
        kern,
        grid=(nc,),
        in_specs=[
            tile(4), tile(5), tile(6), tile(1),
            pl.BlockSpec((8, dh), lambda c: (0, 0)),
            pl.BlockSpec((1, dh), lambda c: (0, 0)),
            pl.BlockSpec(ltri_np.shape, lambda c: (0, 0)),
            pl.BlockSpec(mask_np.shape, lambda c: (0, 0, 0)),
            pl.BlockSpec(rowsel_np.shape, lambda c: (0, 0, 0)),
        ],
        out_specs=tile(0),
        out_shape=jax.ShapeDtypeStruct((bsz, seq, dh), BF16),
        scratch_shapes=[pltpu.VMEM((bsz, heads, HGRN_HEAD_DIM, HGRN_HEAD_DIM), F32),
                        pltpu.VMEM((bsz, C, dh), F32)],
        compiler_params=_cparams(("arbitrary",)),
        name="hgrn2",
    )(pb, pb, pb, pf, lbp, nw, jnp.asarray(ltri_np, BF16), jnp.asarray(mask_np, F32),
      jnp.asarray(rowsel_np, F32))
    return out.reshape(T, dh)


def _s5_tables(a_re, a_im, log_dt, b_re, b_im, c_re, c_im, d_skip, L, nsteps):
    G, P = a_re.shape
    I = S5_GROUP
    gpt = GROUPS_PER_TILE
    nq = G // gpt
    sp = STATE_PER_TILE
    hp = lax.Precision.HIGHEST
    A = lax.complex(jnp.minimum(a_re.astype(F32), -S5_MIN_NEG), a_im.astype(F32))
    dt = jnp.exp(log_dt.astype(F32))[:, None]
    adt = A * dt
    a_bar = jnp.exp(adt)
    B = lax.complex(b_re.astype(F32), b_im.astype(F32))
    b_bar = ((a_bar - 1.0) / A)[..., None] * B
    Cc = lax.complex(c_re.astype(F32), c_im.astype(F32))
    tau = jnp.arange(L + 1, dtype=F32)
    apow = jnp.exp(adt[:, None, :] * tau[None, :, None])
    eye = jnp.eye(gpt, dtype=F32)

    def ri_lanes(z):
        lead = z.shape[:-2]
        return jnp.concatenate([z.real.reshape(*lead, nq, sp), z.imag.reshape(*lead, nq, sp)], axis=-1)

    kt = jnp.einsum('gip,gtp,gpj->gtij', Cc, apow[:, :L], b_bar, precision=hp).real
    kt = kt.reshape(nq, gpt, L, I, I).transpose(0, 2, 1, 4, 3)
    kbd = kt[:, :, :, :, None, :] * eye[None, None, :, None, :, None]
    kbd = kbd.reshape(nq, L, LANE, LANE).transpose(0, 2, 1, 3).reshape(nq, LANE, L * LANE)
    kstrip = jnp.concatenate([jnp.zeros((nq, LANE, (L - 1) * LANE), F32), kbd], axis=-1).astype(BF16)

    bb = b_bar.reshape(nq, gpt, P, I).transpose(0, 1, 3, 2)
    bb = bb[:, :, :, None, :] * eye[None, :, None, :, None]
    bbase = jnp.concatenate([bb.real.reshape(nq, LANE, sp), bb.imag.reshape(nq, LANE, sp)], axis=-1)
    cc = Cc.reshape(nq, gpt, I, P).transpose(0, 1, 3, 2)
    cc = cc[:, :, :, None, :] * eye[None, :, None, :, None]
    cbase = jnp.concatenate([cc.real.reshape(nq, sp, LANE), cc.imag.reshape(nq, sp, LANE)], axis=1)

    aprow = ri_lanes(apow[:, L - 1 - jnp.arange(L)].transpose(1, 0, 2)).transpose(1, 0, 2)
    apc = ri_lanes(apow[:, 1:].transpose(1, 0, 2)).transpose(1, 2, 0)
    apcol = jnp.concatenate([apc, jnp.zeros((nq, 2 * sp, LANE - L), F32)], axis=-1)

    steps = (L * (2.0 ** jnp.arange(nsteps, dtype=F32)))
    alp = ri_lanes(jnp.exp(adt[None] * steps[:, None, None])).transpose(1, 0, 2)
    within = L * (1.0 + jnp.arange(SUBLANES, dtype=F32))
    apsub = ri_lanes(jnp.exp(adt[None] * within[:, None, None])).transpose(1, 0, 2)
    apsub = jnp.tile(apsub, (1, (2 ** nsteps) // SUBLANES, 1))

    dflat = jnp.tile(d_skip.astype(F32).reshape(nq, 1, LANE), (1, L, 1)).reshape(nq, 1, L * LANE)
    return kstrip, bbase, cbase, aprow, apcol, alp, apsub, dflat


def _s5_expand(k_ref, bb_ref, cb_ref, ar_ref, ac_ref, t_scr, b_scr, c_scr, *, L):
    sp = STATE_PER_TILE
    br = bb_ref[0, :, :sp]
    bi = bb_ref[0, :, sp:]
    cr = cb_ref[0, :sp, :]
    ci = cb_ref[0, sp:, :]
    for t in range(L):
        rows = slice(t * LANE, (t + 1) * LANE)
        off = (L - 1 - t) * LANE
        t_scr[rows, :] = k_ref[0, :, off:off + L * LANE]
        ar = ar_ref[0, t:t + 1, :sp]
        ai = ar_ref[0, t:t + 1, sp:]
        b_scr[rows, :sp] = (br * ar - bi * ai).astype(BF16)
        b_scr[rows, sp:] = (br * ai + bi * ar).astype(BF16)
        acr = ac_ref[0, :sp, t:t + 1]
        aci = ac_ref[0, sp:, t:t + 1]
        c_scr[:sp, rows] = (cr * acr - ci * aci).astype(BF16)
        c_scr[sp:, rows] = (-(cr * aci + ci * acr)).astype(BF16)


def _s5_kernel(u_ref, k_ref, bb_ref, cb_ref, ar_ref, ac_ref, al_ref, ap_ref, d_ref, z_ref,
               carry_ref, t_scr, b_scr, c_scr, loc_scr, ent_scr, *, R, L, nsteps):
    sp = STATE_PER_TILE

    @pl.when((pl.program_id(1) == 0) & (pl.program_id(2) == 0))
    def _():
        _s5_expand(k_ref, bb_ref, cb_ref, ar_ref, ac_ref, t_scr, b_scr, c_scr, L=L)

    @pl.when(pl.program_id(2) == 0)
    def _():
        carry_ref[...] = jnp.zeros_like(carry_ref)

    u32 = jnp.concatenate([u_ref[pl.ds(t, R, stride=L), :] for t in range(L)], axis=-1)
    u = u32.astype(BF16)
    z = _dot(u, b_scr[...])
    re = z[:, :sp]
    im = z[:, sp:]
    cre = carry_ref[:, :sp]
    cim = carry_ref[:, sp:]
    G = R // SUBLANES

    def scan_steps(re, im, pos, k0, nk):
        for i in range(nk):
            d = 1 << i
            p_re = al_ref[0, k0 + i:k0 + i + 1, :sp]
            p_im = al_ref[0, k0 + i:k0 + i + 1, sp:]
            keep = pos >= d
            s_re = jnp.where(keep, pltpu.roll(re, d, axis=0), 0.0)
            s_im = jnp.where(keep, pltpu.roll(im, d, axis=0), 0.0)
            re, im = re + p_re * s_re - p_im * s_im, im + p_re * s_im + p_im * s_re
        return re, im

    row = lax.broadcasted_iota(jnp.int32, (R, sp), 0)
    nk1 = int(math.log2(SUBLANES))
    re, im = scan_steps(re, im, row & (SUBLANES - 1), 0, nk1)
    ntile = 2 * sp // LANE
    for c in range(ntile):
        src = re if c < ntile // 2 else im
        c0 = (c % (ntile // 2)) * LANE
        loc_scr[c] = src[:, c0:c0 + LANE]
    ends = jnp.concatenate([loc_scr[c, pl.ds(SUBLANES - 1, G, stride=SUBLANES), :]
                            for c in range(ntile)], axis=-1)
    grow = lax.broadcasted_iota(jnp.int32, (G, sp), 0)
    gfirst = grow == 0
    a_re = al_ref[0, nk1:nk1 + 1, :sp]
    a_im = al_ref[0, nk1:nk1 + 1, sp:]
    e_re = ends[:, :sp] + jnp.where(gfirst, a_re * cre - a_im * cim, 0.0)
    e_im = ends[:, sp:] + jnp.where(gfirst, a_re * cim + a_im * cre, 0.0)
    e_re, e_im = scan_steps(e_re, e_im, grow, nk1, nsteps - nk1)
    ent = jnp.concatenate([jnp.where(gfirst, cre, pltpu.roll(e_re, 1, axis=0)),
                           jnp.where(gfirst, cim, pltpu.roll(e_im, 1, axis=0))], axis=-1)
    for c in range(ntile):
        for j in range(SUBLANES):
            ent_scr[c, pl.ds(j, G, stride=SUBLANES), :] = ent[:, c * LANE:(c + 1) * LANE]
    b_re = jnp.concatenate([ent_scr[c] for c in range(ntile // 2)], axis=-1)
    b_im = jnp.concatenate([ent_scr[c] for c in range(ntile // 2, ntile)], axis=-1)
    p_re = ap_ref[0, :, :sp]
    p_im = ap_ref[0, :, sp:]
    re, im = re + p_re * b_re - p_im * b_im, im + p_re * b_im + p_im * b_re
    carry_ref[:, :sp] = e_re[G - 1:G, :]
    carry_ref[:, sp:] = e_im[G - 1:G, :]
    first = row == 0
    h_re = jnp.where(first, cre, pltpu.roll(re, 1, axis=0))
    h_im = jnp.where(first, cim, pltpu.roll(im, 1, axis=0))
    hprev = jnp.concatenate([h_re, h_im], axis=-1).astype(BF16)
    wide = 2 * LANE
    intra = jnp.concatenate(
        [_dot(u[:, :c0 + wide], t_scr[:c0 + wide, c0:c0 + wide]) for c0 in range(0, L * LANE, wide)],
        axis=-1)
    y = intra + _dot(hprev, c_scr[...]) + d_ref[0] * u32
    zz = 0.5 * y * (1.0 + jnp.tanh(math.sqrt(2.0 / math.pi) * (y + 0.044715 * (y * y * y))))
    for t in range(L):
        z_ref[pl.ds(t, R, stride=L), :] = zz[:, t * LANE:(t + 1) * LANE]


def _s5(proj_f, u_tile0, tables, bsz, seq):
    kstrip, bbase, cbase, aprow, apcol, alp, apsub, dflat = tables
    nq = kstrip.shape[0]
    T = proj_f.shape[0]
    L = aprow.shape[1]
    sp2 = 2 * STATE_PER_TILE
    nb = seq // L
    R = min(S5_ROWS, nb)
    assert R % SUBLANES == 0 and R == apsub.shape[1]
    nrb = nb // R
    nsteps = alp.shape[1]
    kern = functools.partial(_s5_kernel, R=R, L=L, nsteps=nsteps)
    per_q = lambda shape: pl.BlockSpec((1,) + shape, lambda q, b, r: (q, 0, 0))
    return pl.pallas_call(
        kern,
        grid=(nq, bsz, nrb),
        in_specs=[
            pl.BlockSpec((R * L, LANE), lambda q, b, r: (b * nrb + r, u_tile0 + q)),
            per_q((LANE, (2 * L - 1) * LANE)),
            per_q((LANE, sp2)),
            per_q((sp2, LANE)),
            per_q((L, sp2)),
            per_q((sp2, LANE)),
            per_q((nsteps, sp2)),
            per_q((R, sp2)),
            per_q((1, L * LANE)),
        ],
        out_specs=pl.BlockSpec((R * L, LANE), lambda q, b, r: (b * nrb + r, q)),
        out_shape=jax.ShapeDtypeStruct((T, nq * LANE), F32),
        scratch_shapes=[
            pltpu.VMEM((1, sp2), F32),
            pltpu.VMEM((L * LANE, L * LANE), BF16),
            pltpu.VMEM((L * LANE, sp2), BF16),
            pltpu.VMEM((sp2, L * LANE), BF16),
            pltpu.VMEM((sp2 // LANE, R, LANE), F32),
            pltpu.VMEM((sp2 // LANE, R, LANE), F32),
        ],
        compiler_params=_cparams(("arbitrary", "arbitrary", "arbitrary")),
        name="s5",
    )(proj_f, kstrip, bbase, cbase, aprow, apcol, alp, apsub, dflat)


def _merge_out_kernel(x_ref, ya_ref, z_ref, ga_ref, gb_ref, wglu_ref, wa_ref, wb_ref, wo_ref, o_ref):
    z = z_ref[...]
    yb = (z * _sigmoid(_dot(z.astype(BF16), wglu_ref[...]))).astype(BF16)
    ga = _sigmoid(ga_ref[...].astype(F32))
    gb = _sigmoid(gb_ref[...].astype(F32))
    m = ga * _dot(ya_ref[...], wa_ref[...]) + gb * _dot(yb, wb_ref[...])
    o_ref[...] = x_ref[...] + _dot(m.astype(BF16), wo_ref[...])


def _merge_out(x, ya, z, proj_b, wglu, wa, wb, wout):
    T, D = x.shape
    dh = ya.shape[1]
    ds5 = z.shape[1]
    tm = min(512, T)
    full = lambda a: pl.BlockSpec(a.shape, lambda i: (0, 0))
    return pl.pallas_call(
        _merge_out_kernel,
        grid=(T // tm,),
        in_specs=[
            pl.BlockSpec((tm, D), lambda i: (i, 0)),
            pl.BlockSpec((tm, dh), lambda i: (i, 0)),
            pl.BlockSpec((tm, ds5), lambda i: (i, 0)),
            pl.BlockSpec((tm, D), lambda i: (i, 0)),
            pl.BlockSpec((tm, D), lambda i: (i, 1)),
            full(wglu), full(wa), full(wb), full(wout),
        ],
        out_specs=pl.BlockSpec((tm, D), lambda i: (i, 0)),
        out_shape=jax.ShapeDtypeStruct((T, D), F32),
        compiler_params=_cparams(("arbitrary",)),
        name="merge_out",
    )(x, ya, z, proj_b, proj_b, wglu, wa, wb, wout)


def _dense_ffn_kernel(x_ref, nw_ref, wg_ref, wu_ref, wd_ref, o_ref, h_scr):
    @pl.when(pl.program_id(1) == 0)
    def _():
        x = x_ref[...]
        h_scr[...] = _rmsnorm(x, nw_ref[...]).astype(BF16)
        o_ref[...] = x

    h = h_scr[...]
    g = _dot(h, wg_ref[...])
    u = _dot(h, wu_ref[...])
    a = (g * _sigmoid(g) * u).astype(BF16)
    o_ref[...] += _dot(a, wd_ref[...])


def _dense_ffn(x, norm_w, wg, wu, wd):
    T, D = x.shape
    F = wg.shape[1]
    tm = min(1024, T)
    tf = 512 if F % 512 == 0 else F
    return pl.pallas_call(
        _dense_ffn_kernel,
        grid=(T // tm, F // tf),
        in_specs=[
            pl.BlockSpec((tm, D), lambda i, f: (i, 0)),
            pl.BlockSpec((1, D), lambda i, f: (0, 0)),
            pl.BlockSpec((D, tf), lambda i, f: (0, f)),
            pl.BlockSpec((D, tf), lambda i, f: (0, f)),
            pl.BlockSpec((tf, D), lambda i, f: (f, 0)),
        ],
        out_specs=pl.BlockSpec((tm, D), lambda i, f: (i, 0)),
        out_shape=jax.ShapeDtypeStruct((T, D), F32),
        scratch_shapes=[pltpu.VMEM((tm, D), BF16)],
        compiler_params=_cparams(("arbitrary", "arbitrary")),
        name="dense_ffn",
    )(x, norm_w.reshape(1, D), wg, wu, wd)


def _router_kernel(x_ref, nw_ref, whi_ref, wlo_ref, tri_ref, meta_ref, cnt_ref, run_ref, *, n_exp):
    i = pl.program_id(0)

    @pl.when(i == 0)
    def _():
        run_ref[...] = jnp.zeros_like(run_ref)

    h = _rmsnorm(x_ref[...], nw_ref[...])
    hi = h.astype(BF16)
    lo = (h - hi.astype(F32)).astype(BF16)
    logits = _dot(hi, whi_ref[...]) + _dot(lo, whi_ref[...]) + _dot(hi, wlo_ref[...])
    tm = logits.shape[0]
    lane = lax.broadcasted_iota(jnp.int32, (tm, LANE), 1)
    neg = jnp.float32(-jnp.inf)
    l1 = jnp.where(lane < n_exp, logits, neg)
    m1 = jnp.max(l1, axis=-1, keepdims=True)
    i1 = jnp.min(jnp.where(l1 == m1, lane, LANE), axis=-1, keepdims=True)
    l2 = jnp.where(lane == i1, neg, l1)
    m2 = jnp.max(l2, axis=-1, keepdims=True)
    i2 = jnp.min(jnp.where(l2 == m2, lane, LANE), axis=-1, keepdims=True)
    g1 = 1.0 / (1.0 + jnp.exp(m2 - m1))
    g2 = 1.0 - g1
    sel1 = lane == i1
    sel2 = lane == i2
    twohot = (sel1 | sel2).astype(F32)
    before = _dot(tri_ref[...], twohot.astype(BF16)) + run_ref[...]
    p1 = jnp.sum(jnp.where(sel1, before, 0.0), axis=-1, keepdims=True)
    p2 = jnp.sum(jnp.where(sel2, before, 0.0), axis=-1, keepdims=True)
    run = run_ref[...] + jnp.sum(twohot, axis=0, keepdims=True)
    run_ref[...] = run
    cnt_ref[...] = jnp.broadcast_to(run, cnt_ref.shape)
    meta = jnp.where(lane == 0, i1.astype(F32), 0.0)
    meta = jnp.where(lane == 1, i2.astype(F32), meta)
    meta = jnp.where(lane == 2, g1, meta)
    meta = jnp.where(lane == 3, g2, meta)
    meta = jnp.where(lane == 4, p1, meta)
    meta = jnp.where(lane == 5, p2, meta)
    meta_ref[...] = meta


def _router(x, norm_w, w_router):
    T, D = x.shape
    n_exp = w_router.shape[1]
    tm = min(512, T)
    wpad = jnp.zeros((D, LANE), F32).at[:, :n_exp].set(w_router.astype(F32))
    whi = wpad.astype(BF16)
    wlo = (wpad - whi.astype(F32)).astype(BF16)
    tri = jnp.asarray(np.tril(np.ones((tm, tm), np.float32), -1), BF16)
    kern = functools.partial(_router_kernel, n_exp=n_exp)
    meta, cnt = pl.pallas_call(
        kern,
        grid=(T // tm,),
        in_specs=[
            pl.BlockSpec((tm, D), lambda i: (i, 0)),
            pl.BlockSpec((1, D), lambda i: (0, 0)),
            pl.BlockSpec((D, LANE), lambda i: (0, 0)),
            pl.BlockSpec((D, LANE), lambda i: (0, 0)),
            pl.BlockSpec((tm, tm), lambda i: (0, 0)),
        ],
        out_specs=[
            pl.BlockSpec((tm, LANE), lambda i: (i, 0)),
            pl.BlockSpec((8, LANE), lambda i: (0, 0)),
        ],
        out_shape=[
            jax.ShapeDtypeStruct((T, LANE), F32),
            jax.ShapeDtypeStruct((8, LANE), F32),
        ],
        scratch_shapes=[pltpu.VMEM((1, LANE), F32)],
        compiler_params=_cparams(("arbitrary",)),
        name="router",
    )(x, norm_w.reshape(1, D), whi, wlo, tri)
    return meta, cnt[0, :n_exp]


def _zero_fill_pads(padlo_ref, padlen_ref, xs_ref, z_scr, sem, *, n_exp, blk, n_blocks):
    z_scr[...] = jnp.zeros_like(z_scr)
    sub = 8
    bits = [1 << k for k in reversed(range(3, int(math.log2(blk))))]

    def pad_copies(run):
        for e in range(n_exp):
            lo = padlo_ref[e]
            ln = padlen_ref[e]
            head = (sub - lo % sub) % sub
            for r in range(sub - 1):
                @pl.when(r < head)
                def _(r=r):
                    run(pltpu.make_async_copy(z_scr.at[pl.ds(0, 1)], xs_ref.at[pl.ds(lo + r, 1)], sem))

            off = lo + head
            rem = ln - head
            for sz in bits:
                take = (rem & sz) != 0

                @pl.when(take)
                def _(off=off, sz=sz):
                    dst = xs_ref.at[pl.ds(pl.multiple_of(off, sub), sz)]
                    run(pltpu.make_async_copy(z_scr.at[pl.ds(0, sz)], dst, sem))

                off = off + jnp.where(take, sz, 0)

        def tail(b, c):
            @pl.when(b * blk >= padlo_ref[n_exp])
            def _():
                dst = xs_ref.at[pl.ds(pl.multiple_of(b * blk, blk), blk)]
                run(pltpu.make_async_copy(z_scr, dst, sem))
            return c

        lax.fori_loop(0, n_blocks, tail, 0)

    pad_copies(lambda cp: cp.start())
    pad_copies(lambda cp: cp.wait())


def _dispatch_kernel(padlo_ref, padlen_ref, d1_ref, d2_ref, x_ref, nw_ref, xs_ref, h_scr, z_scr,
                     sem, zsem, *, tm, n_exp, blk, n_blocks):
    @pl.when(pl.program_id(0) == 0)
    def _():
        _zero_fill_pads(padlo_ref, padlen_ref, xs_ref, z_scr, zsem,
                        n_exp=n_exp, blk=blk, n_blocks=n_blocks)

    i = pl.program_id(0)
    slot = i % 2
    h_scr[slot] = _rmsnorm(x_ref[...], nw_ref[...])

    def start(t, c):
        for d_ref in (d1_ref, d2_ref):
            pltpu.make_async_copy(h_scr.at[slot, pl.ds(t, 1)], xs_ref.at[pl.ds(d_ref[0, 0, t], 1)],
                                  sem.at[slot]).start()
        return c

    lax.fori_loop(0, tm, start, 0, unroll=8)

    def drain(s):
        for _ in range(TOP_K):
            pltpu.make_async_copy(h_scr.at[s], xs_ref.at[pl.ds(0, tm)], sem.at[s]).wait()

    @pl.when(i > 0)
    def _():
        drain(1 - slot)

    @pl.when(i == pl.num_programs(0) - 1)
    def _():
        drain(slot)


def _dispatch(x, norm_w, dest1, dest2, pad_lo, pad_len, n_rows, blk):
    T, D = x.shape
    tm = min(256, T)
    nb = T // tm
    n_exp = pad_len.shape[0]
    kern = functools.partial(_dispatch_kernel, tm=tm, n_exp=n_exp, blk=blk, n_blocks=n_rows // blk)
    smem_spec = pl.BlockSpec((1, 1, tm), lambda i, lo, ln: (i, 0, 0), memory_space=pltpu.SMEM)
    grid_spec = pltpu.PrefetchScalarGridSpec(
        num_scalar_prefetch=2,
        grid=(nb,),
        in_specs=[
            smem_spec,
            smem_spec,
            pl.BlockSpec((tm, D), lambda i, lo, ln: (i, 0)),
            pl.BlockSpec((1, D), lambda i, lo, ln: (0, 0)),
        ],
        out_specs=pl.BlockSpec(memory_space=pl.ANY),
        scratch_shapes=[pltpu.VMEM((2, tm, D), F32), pltpu.VMEM((blk, D), F32),
                        pltpu.SemaphoreType.DMA((2,)), pltpu.SemaphoreType.DMA(())],
    )
    return pl.pallas_call(
        kern,
        grid_spec=grid_spec,
        out_shape=jax.ShapeDtypeStruct((n_rows, D), F32),
        compiler_params=_cparams(("arbitrary",)),
        name="dispatch",
    )(pad_lo, pad_len, dest1.reshape(nb, 1, tm), dest2.reshape(nb, 1, tm), x, norm_w.reshape(1, D))


def _expert_kernel(exp_ref, nv_ref, xs_ref, wg_ref, wu_ref, wd_ref, y_ref, xb_scr):
    b = pl.program_id(0)
    nv = nv_ref[b]

    @pl.when(pl.program_id(1) == 0)
    def _():
        xb_scr[...] = xs_ref[...].astype(BF16)
        y_ref[...] = jnp.zeros_like(y_ref)

    @pl.when(nv > 0)
    def _():
        xb = xb_scr[...]
        g = _dot(xb, wg_ref[0])
        u = _dot(xb, wu_ref[0])
        a = (g * _sigmoid(g) * u).astype(BF16)
        y_ref[...] += _dot(a, wd_ref[0])


def _experts(xs, blk_exp, blk_nv, wg, wu, wd):
    n_rows, D = xs.shape
    F = wg.shape[2]
    tm = MOE_BLOCK
    tf = 512 if F % 512 == 0 else F
    grid_spec = pltpu.PrefetchScalarGridSpec(
        num_scalar_prefetch=2,
        grid=(n_rows // tm, F // tf),
        in_specs=[
            pl.BlockSpec((tm, D), lambda b, f, e, n: (b, 0)),
            pl.BlockSpec((1, D, tf), lambda b, f, e, n: (e[b], 0, f)),
            pl.BlockSpec((1, D, tf), lambda b, f, e, n: (e[b], 0, f)),
            pl.BlockSpec((1, tf, D), lambda b, f, e, n: (e[b], f, 0)),
        ],
        out_specs=pl.BlockSpec((tm, D), lambda b, f, e, n: (b, 0)),
        scratch_shapes=[pltpu.VMEM((tm, D), BF16)],
    )
    return pl.pallas_call(
        _expert_kernel,
        grid_spec=grid_spec,
        out_shape=jax.ShapeDtypeStruct((n_rows, D), F32),
        compiler_params=_cparams(("arbitrary", "arbitrary")),
        name="experts",
    )(blk_exp, blk_nv, xs, wg, wu, wd)


def _combine_kernel(d1_ref, d2_ref, n1_ref, n2_ref, x_ref, meta_ref, nw_ref, y_ref, o_ref, ybuf, sem,
                    *, tm, final_norm):
    i = pl.program_id(0)
    slot = i % 2

    def gather(r1_ref, r2_ref, s):
        def start(t, c):
            for k, r_ref in enumerate((r1_ref, r2_ref)):
                pltpu.make_async_copy(y_ref.at[pl.ds(r_ref[0, 0, t], 1)],
                                      ybuf.at[s, k, pl.ds(t, 1)], sem.at[s]).start()
            return c

        lax.fori_loop(0, tm, start, 0, unroll=8)

    @pl.when(i == 0)
    def _():
        gather(d1_ref, d2_ref, 0)

    @pl.when(i + 1 < pl.num_programs(0))
    def _():
        gather(n1_ref, n2_ref, 1 - slot)

    for k in range(TOP_K):
        pltpu.make_async_copy(y_ref.at[pl.ds(0, tm)], ybuf.at[slot, k], sem.at[slot]).wait()
    meta = meta_ref[...]
    g1 = meta[:, 2:3]
    g2 = meta[:, 3:4]
    out = x_ref[...] + (g1 * ybuf[slot, 0] + g2 * ybuf[slot, 1])
    if final_norm:
        out = _rmsnorm(out, nw_ref[...])
    o_ref[...] = out


def _combine(x, y, meta, dest1, dest2, final_w):
    T, D = x.shape
    tm = min(256, T)
    nb = T // tm
    final_norm = final_w is not None
    nw = (final_w if final_norm else jnp.ones((D,), F32)).reshape(1, D)
    kern = functools.partial(_combine_kernel, tm=tm, final_norm=final_norm)
    cur_spec = pl.BlockSpec((1, 1, tm), lambda i: (i, 0, 0), memory_space=pltpu.SMEM)
    nxt_spec = pl.BlockSpec((1, 1, tm), lambda i: (jnp.minimum(i + 1, nb - 1), 0, 0),
                            memory_space=pltpu.SMEM)
    d1 = dest1.reshape(nb, 1, tm)
    d2 = dest2.reshape(nb, 1, tm)
    return pl.pallas_call(
        kern,
        grid=(nb,),
        in_specs=[
            cur_spec,
            cur_spec,
            nxt_spec,
            nxt_spec,
            pl.BlockSpec((tm, D), lambda i: (i, 0)),
            pl.BlockSpec((tm, LANE), lambda i: (i, 0)),
            pl.BlockSpec((1, D), lambda i: (0, 0)),
            pl.BlockSpec(memory_space=pl.ANY),
        ],
        out_specs=pl.BlockSpec((tm, D), lambda i: (i, 0)),
        out_shape=jax.ShapeDtypeStruct((T, D), F32),
        scratch_shapes=[pltpu.VMEM((2, TOP_K, tm, D), F32), pltpu.SemaphoreType.DMA((2,))],
        compiler_params=_cparams(("arbitrary",)),
        name="combine",
    )(d1, d2, d1, d2, x, meta, nw, y)


def _moe(x, norm_w, w_router, wg, wu, wd, final_w):
    T, D = x.shape
    n_exp = w_router.shape[1]
    blk = MOE_BLOCK
    meta, counts_f = _router(x, norm_w, w_router)
    counts = counts_f.astype(jnp.int32)
    padded = (counts + blk - 1) // blk * blk
    pad_ends = jnp.cumsum(padded)
    pad_starts = pad_ends - padded
    n_rows = (T * TOP_K + blk - 1) // blk * blk + n_exp * blk
    n_blocks = n_rows // blk
    e1 = meta[:, 0].astype(jnp.int32)
    e2 = meta[:, 1].astype(jnp.int32)
    eids = jnp.arange(n_exp, dtype=jnp.int32)[None, :]
    start_of = lambda e: jnp.sum(jnp.where(e[:, None] == eids, pad_starts[None, :], 0), axis=1)
    dest1 = start_of(e1) + meta[:, 4].astype(jnp.int32)
    dest2 = start_of(e2) + meta[:, 5].astype(jnp.int32)
    blk_start = jnp.arange(n_blocks, dtype=jnp.int32) * blk
    blk_exp = jnp.minimum(jnp.sum((pad_ends[None, :] <= blk_start[:, None]).astype(jnp.int32), axis=1),
                          n_exp - 1)
    blk_nv = jnp.clip(pad_starts[blk_exp] + counts[blk_exp] - blk_start, 0, blk).astype(jnp.int32)
    pad_lo = jnp.concatenate([pad_starts + counts, pad_ends[-1:]]).astype(jnp.int32)
    pad_len = (padded - counts).astype(jnp.int32)
    xs = _dispatch(x, norm_w, dest1, dest2, pad_lo, pad_len, n_rows, blk)
    y = _experts(xs, blk_exp, blk_nv, wg, wu, wd)
    return _combine(x, y, meta, dest1, dest2, final_w)


def _final_norm_kernel(x_ref, nw_ref, o_ref):
    o_ref[...] = _rmsnorm(x_ref[...], nw_ref[...])


def _final_norm(x, w):
    T, D = x.shape
    tm = min(1024, T)
    return pl.pallas_call(
        _final_norm_kernel,
        grid=(T // tm,),
        in_specs=[pl.BlockSpec((tm, D), lambda i: (i, 0)), pl.BlockSpec((1, D), lambda i: (0, 0))],
        out_specs=pl.BlockSpec((tm, D), lambda i: (i, 0)),
        out_shape=jax.ShapeDtypeStruct((T, D), F32),
        compiler_params=_cparams(("arbitrary",)),
        name="final_norm",
    )(x, w.reshape(1, D))


def kernel(x, attn_norm_w, w_in, hgrn_lb_logits, hgrn_norm_w, s5_a_re, s5_a_im, s5_log_dt,
           s5_b_re, s5_b_im, s5_c_re, s5_c_im, s5_d, s5_w_glu, w_branch_a, w_branch_b,
           w_out, ffn_norm_w, dense_w_gate, dense_w_up, dense_w_down, moe_w_router,
           moe_w_gate, moe_w_up, moe_w_down, final_norm_w):
    bsz, seq, D = x.shape
    depth = w_in.shape[0]
    dh = hgrn_lb_logits.shape[1]
    T = bsz * seq
    lower = jnp.cumsum(jax.nn.softmax(hgrn_lb_logits.astype(F32), axis=0), axis=0)
    lower = lower - lower[0]
    nb = seq // S5_L
    nsteps = max(1, int(math.log2(min(S5_ROWS, nb))))
    xf = x.reshape(T, D).astype(F32)
    for layer in range(depth):
        w = w_in[layer]
        seg = lambda k, n=1: w[:, k * dh:(k + n) * dh]
        w_r = jnp.concatenate([seg(5, 2), seg(7, 2), seg(0), seg(2), seg(3), seg(1), seg(4)],
                              axis=1).astype(BF16)
        proj_b, proj_f = _norm_inproj(xf, attn_norm_w[layer], w_r, dh)
        ya = _hgrn2(proj_b, proj_f, lower[layer], hgrn_norm_w[layer], bsz, seq)
        tables = _s5_tables(s5_a_re[layer], s5_a_im[layer], s5_log_dt[layer], s5_b_re[layer],
                            s5_b_im[layer], s5_c_re[layer], s5_c_im[layer], s5_d[layer],
                            S5_L, nsteps)
        z = _s5(proj_f, 2 * dh // LANE, tables, bsz, seq)
        xf = _merge_out(xf, ya, z, proj_b, s5_w_glu[layer].astype(BF16),
                        w_branch_a[layer].astype(BF16), w_branch_b[layer].astype(BF16),
                        w_out[layer].astype(BF16))
        last = layer == depth - 1
        j = layer // 2
        if layer % 2 == 0:
            xf = _dense_ffn(xf, ffn_norm_w[layer], dense_w_gate[j].astype(BF16),
                            dense_w_up[j].astype(BF16), dense_w_down[j].astype(BF16))
            if last:
                xf = _final_norm(xf, final_norm_w)
        else:
            xf = _moe(xf, ffn_norm_w[layer], moe_w_router[j], moe_w_gate[j].astype(BF16),
                      moe_w_up[j].astype(BF16), moe_w_down[j].astype(BF16),
                      final_norm_w if last else None)
    return xf.reshape(bsz, seq, D).astype(x.dtype)
```

```python
import functools
import math

import numpy as np
import jax
import jax.numpy as jnp
from jax import lax
from jax.experimental import pallas as pl
from jax.experimental.pallas import tpu as pltpu

F32 = jnp.float32
BF16 = jnp.bfloat16

RMS_EPS = 1e-6
HGRN_HEAD_DIM = 128
S5_GROUP = 16
S5_STATE = 64
S5_MIN_NEG = 1e-4
TOP_K = 2

LANE = 128
SUBLANES = 8
VMEM_LIMIT = 58 * 1024 * 1024

HGRN_CHUNK = 128
S5_L = 16
S5_ROWS = 512
MOE_BLOCK = 1024
EXPERT_ROW_PARTS = 4
GROUPS_PER_TILE = LANE // S5_GROUP
STATE_PER_TILE = GROUPS_PER_TILE * S5_STATE


def _cparams(sem):
    return pltpu.CompilerParams(dimension_semantics=sem, vmem_limit_bytes=VMEM_LIMIT)


def _dot(a, b):
    return jnp.dot(a, b, preferred_element_type=F32)


def _dot_nt(a, b):
    return lax.dot_general(a, b, (((1,), (1,)), ((), ())), preferred_element_type=F32)


def _dot_tn(a, b):
    return lax.dot_general(a, b, (((0,), (0,)), ((), ())), preferred_element_type=F32)


def _sigmoid(x):
    return 1.0 / (1.0 + jnp.exp(-x))


def _rmsnorm(x, w):
    ms = jnp.mean(x * x, axis=-1, keepdims=True)
    return x * lax.rsqrt(ms + RMS_EPS) * w


N_BF16_COLS = 7
N_F32_COLS = 2


def _norm_inproj_kernel(x_ref, nw_ref, w_ref, ob_ref, of_ref, h_scr):
    @pl.when(pl.program_id(1) == 0)
    def _():
        h_scr[...] = _rmsnorm(x_ref[...], nw_ref[...]).astype(BF16)

    acc = _dot(h_scr[...], w_ref[...])
    ob_ref[...] = acc.astype(BF16)
    of_ref[...] = acc


def _norm_inproj(x, norm_w, w_r, dh):
    T, D = x.shape
    tm = min(1024, T)
    ncols = N_BF16_COLS + N_F32_COLS
    return pl.pallas_call(
        _norm_inproj_kernel,
        grid=(T // tm, ncols),
        in_specs=[
            pl.BlockSpec((tm, D), lambda i, j: (i, 0)),
            pl.BlockSpec((1, D), lambda i, j: (0, 0)),
            pl.BlockSpec((D, dh), lambda i, j: (0, j)),
        ],
        out_specs=[
            pl.BlockSpec((tm, dh), lambda i, j: (i, jnp.minimum(j, N_BF16_COLS))),
            pl.BlockSpec((tm, dh), lambda i, j: (i, jnp.maximum(j - N_BF16_COLS + 1, 0))),
        ],
        out_shape=[
            jax.ShapeDtypeStruct((T, (N_BF16_COLS + 1) * dh), BF16),
            jax.ShapeDtypeStruct((T, (N_F32_COLS + 1) * dh), F32),
        ],
        scratch_shapes=[pltpu.VMEM((tm, D), BF16)],
        compiler_params=_cparams(("arbitrary", "arbitrary")),
        name="norm_inproj",
    )(x, norm_w.reshape(1, D), w_r)


def _hgrn_tables(C):
    nlev = int(math.log2(C))
    r = np.arange(C)[:, None]
    c = np.arange(C)[None, :]
    tri = (c <= r).astype(np.float32)
    ltri = np.concatenate([tri, tri, tri], axis=1)
    masks = [np.eye(C, dtype=bool)]
    for lev in range(1, nlev + 1):
        b = 2 ** lev
        half = b // 2
        masks.append(((r // b) == (c // b)) & ((r % b) >= half) & ((c % b) < half))
    m = np.stack(masks, axis=0).astype(np.float32)
    rr = np.arange(C)
    rowsel = np.stack([rr % 2 == 1, rr % 4 == 0, rr % 4 >= 2, rr % 4 == 3]).astype(np.float32)
    return ltri, m, rowsel, nlev


LOG2E = 1.4426950408889634


def _hgrn2_chunk(q_pre, v, g_pre, x, lb_ref, nw_ref, ltri_ref, mask_ref, rs_ref, st_ref, p_scr,
                 *, C, nlev, heads):
    hd = HGRN_HEAD_DIM
    dh = x.shape[1]
    e = jnp.exp(-jnp.abs(x))
    log_sig = jnp.minimum(x, 0.0) - jnp.log(1.0 + e)
    a_ = lb_ref[0:1, :]
    b_ = lb_ref[1:2, :] + log_sig
    logf = jnp.maximum(a_, b_) + jnp.log(1.0 + jnp.exp(-jnp.abs(a_ - b_)))
    kk = lb_ref[2:3, :] * jnp.where(x >= 0.0, e, 1.0) / (1.0 + e)
    q = q_pre.astype(F32)
    qs_b = (q * _sigmoid(q)).astype(BF16)
    kk_b = kk.astype(BF16)

    lf2 = logf * LOG2E
    hi = lf2.astype(BF16)
    r1 = lf2 - hi.astype(F32)
    mid = r1.astype(BF16)
    lo = (r1 - mid.astype(F32)).astype(BF16)
    p = _dot(ltri_ref[...], jnp.concatenate([hi, mid, lo], axis=0))
    p_scr[...] = p

    e_q = jnp.exp2(p)
    q_in = qs_b * e_q.astype(BF16)
    dec = e_q[C - 1:C, :]
    k_st = kk_b * jnp.exp2(p_scr[C - 1:C, :] - p).astype(BF16)

    def level_decay(lev):
        if lev == 1:
            return jnp.exp2(lf2 * rs_ref[0])
        if lev == 2:
            win = (pltpu.roll(lf2, C - 1, axis=0) * rs_ref[1] + lf2 * rs_ref[2]
                   + pltpu.roll(lf2, 1, axis=0) * rs_ref[3])
            return jnp.exp2(win)
        b = 2 ** lev
        half = b // 2
        if half % 8:
            refs = [jnp.broadcast_to(p_scr[k * b + half - 1:k * b + half, :], (b, dh))
                    for k in range(C // b)]
            return jnp.exp2(-jnp.abs(p - jnp.concatenate(refs, axis=0)))
        parts = []
        for k in range(C // b):
            ref = jnp.broadcast_to(p_scr[k * b + half - 1:k * b + half, :], (half, dh))
            parts.append(ref - p_scr[k * b:k * b + half, :])
            parts.append(p_scr[k * b + half:(k + 1) * b, :] - ref)
        return jnp.exp2(jnp.concatenate(parts, axis=0))

    scores = [None] * heads
    for lev in range(nlev + 1):
        if lev == 0:
            ql = qs_b
            kl = kk_b
        else:
            e_l = level_decay(lev).astype(BF16)
            ql = qs_b * e_l
            kl = kk_b * e_l
        m = mask_ref[lev]
        for h in range(heads):
            sl = slice(h * hd, (h + 1) * hd)
            part = _dot_nt(ql[:, sl], kl[:, sl]) * m
            scores[h] = part if scores[h] is None else scores[h] + part

    outs = []
    for h in range(heads):
        sl = slice(h * hd, (h + 1) * hd)
        st = st_ref[h]
        o_h = _dot(scores[h].astype(BF16), v[:, sl]) + _dot_nt(q_in[:, sl], st.astype(BF16))
        st_ref[h] = st * dec[:, sl] + _dot_tn(v[:, sl], k_st[:, sl])
        ms = jnp.mean(o_h * o_h, axis=-1, keepdims=True)
        outs.append(o_h * lax.rsqrt(ms + RMS_EPS))
    o = jnp.concatenate(outs, axis=-1)
    g = g_pre.astype(F32)
    return (o * nw_ref[...] * (g * _sigmoid(g))).astype(BF16)


def _hgrn2_kernel(q_ref, i_ref, g_ref, f_ref, lb_ref, nw_ref, ltri_ref, mask_ref, rs_ref,
                  o_ref, st_ref, p_scr, *, C, nlev, heads):
    @pl.when(pl.program_id(0) == 0)
    def _():
        st_ref[...] = jnp.zeros_like(st_ref)

    for b in range(q_ref.shape[0]):
        o_ref[b] = _hgrn2_chunk(q_ref[b], i_ref[b], g_ref[b], f_ref[b], lb_ref, nw_ref, ltri_ref,
                                mask_ref, rs_ref, st_ref.at[b], p_scr.at[b],
                                C=C, nlev=nlev, heads=heads)


def _hgrn2(proj_b, proj_f, lb, norm_w, bsz, seq):
    T = proj_f.shape[0]
    dh = lb.shape[0]
    heads = dh // HGRN_HEAD_DIM
    C = min(HGRN_CHUNK, seq)
    ltri_np, mask_np, rowsel_np, nlev = _hgrn_tables(C)
    rowsel_np = np.ascontiguousarray(np.broadcast_to(rowsel_np[:, :, None], rowsel_np.shape + (dh,)))
    nc = seq // C
    lbp = jnp.stack([jnp.log(lb), jnp.log1p(-lb), 1.0 - lb], axis=0)
    lbp = jnp.concatenate([lbp, jnp.zeros((5, dh), F32)], axis=0)
    nw = jnp.tile(norm_w.astype(F32), heads).reshape(1, dh)
    pb = proj_b.reshape(bsz, seq, proj_b.shape[1])
    pf = proj_f.reshape(bsz, seq, proj_f.shape[1])
    tile = lambda k: pl.BlockSpec((bsz, C, dh), lambda c: (0, c, k))
    kern = functools.partial(_hgrn2_kernel, C=C, nlev=nlev, heads=heads)
    out = pl.pallas_call(
        kern,
        grid=(nc,),
        in_specs=[
            tile(4), tile(5), tile(6), tile(1),
            pl.BlockSpec((8, dh), lambda c: (0, 0)),
            pl.BlockSpec((1, dh), lambda c: (0, 0)),
            pl.BlockSpec(ltri_np.shape, lambda c: (0, 0)),
            pl.BlockSpec(mask_np.shape, lambda c: (0, 0, 0)),
            pl.BlockSpec(rowsel_np.shape, lambda c: (0, 0, 0)),
        ],
        out_specs=tile(0),
        out_shape=jax.ShapeDtypeStruct((bsz, seq, dh), BF16),
        scratch_shapes=[pltpu.VMEM((bsz, heads, HGRN_HEAD_DIM, HGRN_HEAD_DIM), F32),
                        pltpu.VMEM((bsz, C, dh), F32)],
        compiler_params=_cparams(("arbitrary",)),
        name="hgrn2",
    )(pb, pb, pb, pf, lbp, nw, jnp.asarray(ltri_np, BF16), jnp.asarray(mask_np, F32),
      jnp.asarray(rowsel_np, F32))
    return out.reshape(T, dh)


def _s5_tables(a_re, a_im, log_dt, b_re, b_im, c_re, c_im, d_skip, L, nsteps):
    G, P = a_re.shape
    I = S5_GROUP
    gpt = GROUPS_PER_TILE
    nq = G // gpt
    sp = STATE_PER_TILE
    hp = lax.Precision.HIGHEST
    A = lax.complex(jnp.minimum(a_re.astype(F32), -S5_MIN_NEG), a_im.astype(F32))
    dt = jnp.exp(log_dt.astype(F32))[:, None]
    adt = A * dt
    a_bar = jnp.exp(adt)
    B = lax.complex(b_re.astype(F32), b_im.astype(F32))
    b_bar = ((a_bar - 1.0) / A)[..., None] * B
    Cc = lax.complex(c_re.astype(F32), c_im.astype(F32))
    tau = jnp.arange(L + 1, dtype=F32)
    apow = jnp.exp(adt[:, None, :] * tau[None, :, None])

    def ri_lanes(z):
        lead = z.shape[:-2]
        return jnp.concatenate([z.real.reshape(*lead, nq, sp), z.imag.reshape(*lead, nq, sp)], axis=-1)

    kt = jnp.einsum('gip,gtp,gpj->gtij', Cc, apow[:, :L], b_bar, precision=hp).real
    def block_diag(a):
        cdim = a.shape[-1]
        own = (jnp.arange(gpt * cdim) // cdim)[None, :] == jnp.arange(gpt)[:, None]
        tiled = jnp.tile(a, (1, 1, 1, gpt))
        out = jnp.where(own[None, :, None, :], tiled, 0.0)
        return out.reshape(a.shape[0], gpt * a.shape[2], gpt * cdim)

    kt = kt.reshape(nq, gpt, L, I, I).transpose(0, 2, 1, 4, 3)
    kbd = block_diag(kt.reshape(nq * L, gpt, I, I)).reshape(nq, L, LANE, LANE)
    kbd = kbd.transpose(0, 2, 1, 3).reshape(nq, LANE, L * LANE)
    kstrip = jnp.concatenate([jnp.zeros((nq, LANE, (L - 1) * LANE), F32), kbd], axis=-1).astype(BF16)

    bb = b_bar.reshape(nq, gpt, P, I).transpose(0, 1, 3, 2)
    bbase = jnp.concatenate([block_diag(bb.real), block_diag(bb.imag)], axis=-1)
    cc = Cc.reshape(nq, gpt, I, P).transpose(0, 1, 3, 2)
    cbase = jnp.concatenate([block_diag(cc.real), block_diag(cc.imag)], axis=1)

    aprow = ri_lanes(apow[:, L - 1 - jnp.arange(L)].transpose(1, 0, 2)).transpose(1, 0, 2)
    apc = ri_lanes(apow[:, 1:].transpose(1, 0, 2)).transpose(1, 2, 0)
    apcol = jnp.concatenate([apc, jnp.zeros((nq, 2 * sp, LANE - L), F32)], axis=-1)

    steps = (L * (2.0 ** jnp.arange(nsteps, dtype=F32)))
    alp = ri_lanes(jnp.exp(adt[None] * steps[:, None, None])).transpose(1, 0, 2)
    within = L * (1.0 + jnp.arange(SUBLANES, dtype=F32))
    apsub = ri_lanes(jnp.exp(adt[None] * within[:, None, None])).transpose(1, 0, 2)
    apsub = jnp.tile(apsub, (1, (2 ** nsteps) // SUBLANES, 1))

    dflat = jnp.tile(d_skip.astype(F32).reshape(nq, 1, LANE), (1, L, 1)).reshape(nq, 1, L * LANE)
    return kstrip, bbase, cbase, aprow, apcol, alp, apsub, dflat


def _s5_expand(k_ref, bb_ref, cb_ref, ar_ref, ac_ref, t_scr, b_scr, c_scr, *, L):
    sp = STATE_PER_TILE
    br = bb_ref[0, :, :sp]
    bi = bb_ref[0, :, sp:]
    cr = cb_ref[0, :sp, :]
    ci = cb_ref[0, sp:, :]
    for t in range(L):
        rows = slice(t * LANE, (t + 1) * LANE)
        off = (L - 1 - t) * LANE
        t_scr[rows, :] = k_ref[0, :, off:off + L * LANE]
        ar = ar_ref[0, t:t + 1, :sp]
        ai = ar_ref[0, t:t + 1, sp:]
        b_scr[rows, :sp] = (br * ar - bi * ai).astype(BF16)
        b_scr[rows, sp:] = (br * ai + bi * ar).astype(BF16)
        acr = ac_ref[0, :sp, t:t + 1]
        aci = ac_ref[0, sp:, t:t + 1]
        c_scr[:sp, rows] = (cr * acr - ci * aci).astype(BF16)
        c_scr[sp:, rows] = (-(cr * aci + ci * acr)).astype(BF16)


def _s5_kernel(u_ref, k_ref, bb_ref, cb_ref, ar_ref, ac_ref, al_ref, ap_ref, d_ref, z_ref,
               carry_ref, t_scr, b_scr, c_scr, loc_scr, ent_scr, *, R, L, nsteps):
    sp = STATE_PER_TILE

    @pl.when((pl.program_id(1) == 0) & (pl.program_id(2) == 0))
    def _():
        _s5_expand(k_ref, bb_ref, cb_ref, ar_ref, ac_ref, t_scr, b_scr, c_scr, L=L)

    @pl.when(pl.program_id(2) == 0)
    def _():
        carry_ref[...] = jnp.zeros_like(carry_ref)

    u32 = jnp.concatenate([u_ref[pl.ds(t, R, stride=L), :] for t in range(L)], axis=-1)
    u = u32.astype(BF16)
    z = _dot(u, b_scr[...])
    re = z[:, :sp]
    im = z[:, sp:]
    cre = carry_ref[:, :sp]
    cim = carry_ref[:, sp:]
    G = R // SUBLANES

    def scan_steps(re, im, pos, k0, nk):
        for i in range(nk):
            d = 1 << i
            p_re = al_ref[0, k0 + i:k0 + i + 1, :sp]
            p_im = al_ref[0, k0 + i:k0 + i + 1, sp:]
            keep = pos >= d
            s_re = jnp.where(keep, pltpu.roll(re, d, axis=0), 0.0)
            s_im = jnp.where(keep, pltpu.roll(im, d, axis=0), 0.0)
            re, im = re + p_re * s_re - p_im * s_im, im + p_re * s_im + p_im * s_re
        return re, im

    row = lax.broadcasted_iota(jnp.int32, (R, sp), 0)
    nk1 = int(math.log2(SUBLANES))
    re, im = scan_steps(re, im, row & (SUBLANES - 1), 0, nk1)
    ntile = 2 * sp // LANE
    for c in range(ntile):
        src = re if c < ntile // 2 else im
        c0 = (c % (ntile // 2)) * LANE
        loc_scr[c] = src[:, c0:c0 + LANE]
    ends = jnp.concatenate([loc_scr[c, pl.ds(SUBLANES - 1, G, stride=SUBLANES), :]
                            for c in range(ntile)], axis=-1)
    grow = lax.broadcasted_iota(jnp.int32, (G, sp), 0)
    gfirst = grow == 0
    a_re = al_ref[0, nk1:nk1 + 1, :sp]
    a_im = al_ref[0, nk1:nk1 + 1, sp:]
    e_re = ends[:, :sp] + jnp.where(gfirst, a_re * cre - a_im * cim, 0.0)
    e_im = ends[:, sp:] + jnp.where(gfirst, a_re * cim + a_im * cre, 0.0)
    e_re, e_im = scan_steps(e_re, e_im, grow, nk1, nsteps - nk1)
    ent = jnp.concatenate([jnp.where(gfirst, cre, pltpu.roll(e_re, 1, axis=0)),
                           jnp.where(gfirst, cim, pltpu.roll(e_im, 1, axis=0))], axis=-1)
    for c in range(ntile):
        for j in range(SUBLANES):
            ent_scr[c, pl.ds(j, G, stride=SUBLANES), :] = ent[:, c * LANE:(c + 1) * LANE]
    b_re = jnp.concatenate([ent_scr[c] for c in range(ntile // 2)], axis=-1)
    b_im = jnp.concatenate([ent_scr[c] for c in range(ntile // 2, ntile)], axis=-1)
    p_re = ap_ref[0, :, :sp]
    p_im = ap_ref[0, :, sp:]
    re, im = re + p_re * b_re - p_im * b_im, im + p_re * b_im + p_im * b_re
    carry_ref[:, :sp] = e_re[G - 1:G, :]
    carry_ref[:, sp:] = e_im[G - 1:G, :]
    first = row == 0
    h_re = jnp.where(first, cre, pltpu.roll(re, 1, axis=0))
    h_im = jnp.where(first, cim, pltpu.roll(im, 1, axis=0))
    hprev = jnp.concatenate([h_re, h_im], axis=-1).astype(BF16)
    wide = 2 * LANE
    intra = jnp.concatenate(
        [_dot(u[:, :c0 + wide], t_scr[:c0 + wide, c0:c0 + wide]) for c0 in range(0, L * LANE, wide)],
        axis=-1)
    y = intra + _dot(hprev, c_scr[...]) + d_ref[0] * u32
    zz = 0.5 * y * (1.0 + jnp.tanh(math.sqrt(2.0 / math.pi) * (y + 0.044715 * (y * y * y))))
    for t in range(L):
        z_ref[pl.ds(t, R, stride=L), :] = zz[:, t * LANE:(t + 1) * LANE]


def _s5(proj_f, u_tile0, tables, bsz, seq):
    kstrip, bbase, cbase, aprow, apcol, alp, apsub, dflat = tables
    nq = kstrip.shape[0]
    T = proj_f.shape[0]
    L = aprow.shape[1]
    sp2 = 2 * STATE_PER_TILE
    nb = seq // L
    R = min(S5_ROWS, nb)
    assert R % SUBLANES == 0 and R == apsub.shape[1]
    nrb = nb // R
    nsteps = alp.shape[1]
    kern = functools.partial(_s5_kernel, R=R, L=L, nsteps=nsteps)
    per_q = lambda shape: pl.BlockSpec((1,) + shape, lambda q, b, r: (q, 0, 0))
    return pl.pallas_call(
        kern,
        grid=(nq, bsz, nrb),
        in_specs=[
            pl.BlockSpec((R * L, LANE), lambda q, b, r: (b * nrb + r, u_tile0 + q)),
            per_q((LANE, (2 * L - 1) * LANE)),
            per_q((LANE, sp2)),
            per_q((sp2, LANE)),
            per_q((L, sp2)),
            per_q((sp2, LANE)),
            per_q((nsteps, sp2)),
            per_q((R, sp2)),
            per_q((1, L * LANE)),
        ],
        out_specs=pl.BlockSpec((R * L, LANE), lambda q, b, r: (b * nrb + r, q)),
        out_shape=jax.ShapeDtypeStruct((T, nq * LANE), F32),
        scratch_shapes=[
            pltpu.VMEM((1, sp2), F32),
            pltpu.VMEM((L * LANE, L * LANE), BF16),
            pltpu.VMEM((L * LANE, sp2), BF16),
            pltpu.VMEM((sp2, L * LANE), BF16),
            pltpu.VMEM((sp2 // LANE, R, LANE), F32),
            pltpu.VMEM((sp2 // LANE, R, LANE), F32),
        ],
        compiler_params=_cparams(("arbitrary", "arbitrary", "arbitrary")),
        name="s5",
    )(proj_f, kstrip, bbase, cbase, aprow, apcol, alp, apsub, dflat)


def _merge_out_kernel(x_ref, ya_ref, z_ref, ga_ref, gb_ref, wglu_ref, wa_ref, wb_ref, wo_ref, o_ref):
    z = z_ref[...]
    yb = (z * _sigmoid(_dot(z.astype(BF16), wglu_ref[...]))).astype(BF16)
    ga = _sigmoid(ga_ref[...].astype(F32))
    gb = _sigmoid(gb_ref[...].astype(F32))
    m = ga * _dot(ya_ref[...], wa_ref[...]) + gb * _dot(yb, wb_ref[...])
    o_ref[...] = x_ref[...] + _dot(m.astype(BF16), wo_ref[...])


def _merge_out(x, ya, z, proj_b, wglu, wa, wb, wout):
    T, D = x.shape
    dh = ya.shape[1]
    ds5 = z.shape[1]
    tm = min(512, T)
    full = lambda a: pl.BlockSpec(a.shape, lambda i: (0, 0))
    return pl.pallas_call(
        _merge_out_kernel,
        grid=(T // tm,),
        in_specs=[
            pl.BlockSpec((tm, D), lambda i: (i, 0)),
            pl.BlockSpec((tm, dh), lambda i: (i, 0)),
            pl.BlockSpec((tm, ds5), lambda i: (i, 0)),
            pl.BlockSpec((tm, D), lambda i: (i, 0)),
            pl.BlockSpec((tm, D), lambda i: (i, 1)),
            full(wglu), full(wa), full(wb), full(wout),
        ],
        out_specs=pl.BlockSpec((tm, D), lambda i: (i, 0)),
        out_shape=jax.ShapeDtypeStruct((T, D), F32),
        compiler_params=_cparams(("arbitrary",)),
        name="merge_out",
    )(x, ya, z, proj_b, proj_b, wglu, wa, wb, wout)


def _dense_ffn_kernel(x_ref, nw_ref, wg_ref, wu_ref, wd_ref, o_ref, h_scr):
    @pl.when(pl.program_id(1) == 0)
    def _():
        x = x_ref[...]
        h_scr[...] = _rmsnorm(x, nw_ref[...]).astype(BF16)
        o_ref[...] = x

    h = h_scr[...]
    g = _dot(h, wg_ref[...])
    u = _dot(h, wu_ref[...])
    a = (g * _sigmoid(g) * u).astype(BF16)
    o_ref[...] += _dot(a, wd_ref[...])


def _dense_ffn(x, norm_w, wg, wu, wd):
    T, D = x.shape
    F = wg.shape[1]
    tm = min(1024, T)
    tf = 512 if F % 512 == 0 else F
    return pl.pallas_call(
        _dense_ffn_kernel,
        grid=(T // tm, F // tf),
        in_specs=[
            pl.BlockSpec((tm, D), lambda i, f: (i, 0)),
            pl.BlockSpec((1, D), lambda i, f: (0, 0)),
            pl.BlockSpec((D, tf), lambda i, f: (0, f)),
            pl.BlockSpec((D, tf), lambda i, f: (0, f)),
            pl.BlockSpec((tf, D), lambda i, f: (f, 0)),
        ],
        out_specs=pl.BlockSpec((tm, D), lambda i, f: (i, 0)),
        out_shape=jax.ShapeDtypeStruct((T, D), F32),
        scratch_shapes=[pltpu.VMEM((tm, D), BF16)],
        compiler_params=_cparams(("arbitrary", "arbitrary")),
        name="dense_ffn",
    )(x, norm_w.reshape(1, D), wg, wu, wd)


def _router_kernel(x_ref, nw_ref, whi_ref, wlo_ref, tri_ref, meta_ref, cnt_ref, run_ref, *, n_exp):
    i = pl.program_id(0)

    @pl.when(i == 0)
    def _():
        run_ref[...] = jnp.zeros_like(run_ref)

    h = _rmsnorm(x_ref[...], nw_ref[...])
    hi = h.astype(BF16)
    lo = (h - hi.astype(F32)).astype(BF16)
    logits = _dot(hi, whi_ref[...]) + _dot(lo, whi_ref[...]) + _dot(hi, wlo_ref[...])
    tm = logits.shape[0]
    lane = lax.broadcasted_iota(jnp.int32, (tm, LANE), 1)
    neg = jnp.float32(-jnp.inf)
    l1 = jnp.where(lane < n_exp, logits, neg)
    m1 = jnp.max(l1, axis=-1, keepdims=True)
    i1 = jnp.min(jnp.where(l1 == m1, lane, LANE), axis=-1, keepdims=True)
    l2 = jnp.where(lane == i1, neg, l1)
    m2 = jnp.max(l2, axis=-1, keepdims=True)
    i2 = jnp.min(jnp.where(l2 == m2, lane, LANE), axis=-1, keepdims=True)
    g1 = 1.0 / (1.0 + jnp.exp(m2 - m1))
    g2 = 1.0 - g1
    sel1 = lane == i1
    sel2 = lane == i2
    twohot = (sel1 | sel2).astype(F32)
    before = _dot(tri_ref[...], twohot.astype(BF16)) + run_ref[...]
    p1 = jnp.sum(jnp.where(sel1, before, 0.0), axis=-1, keepdims=True)
    p2 = jnp.sum(jnp.where(sel2, before, 0.0), axis=-1, keepdims=True)
    run = run_ref[...] + jnp.sum(twohot, axis=0, keepdims=True)
    run_ref[...] = run
    cnt_ref[...] = jnp.broadcast_to(run, cnt_ref.shape)
    meta = jnp.where(lane == 0, i1.astype(F32), 0.0)
    meta = jnp.where(lane == 1, i2.astype(F32), meta)
    meta = jnp.where(lane == 2, g1, meta)
    meta = jnp.where(lane == 3, g2, meta)
    meta = jnp.where(lane == 4, p1, meta)
    meta = jnp.where(lane == 5, p2, meta)
    meta_ref[...] = meta


def _router(x, norm_w, w_router):
    T, D = x.shape
    n_exp = w_router.shape[1]
    tm = min(512, T)
    wpad = jnp.zeros((D, LANE), F32).at[:, :n_exp].set(w_router.astype(F32))
    whi = wpad.astype(BF16)
    wlo = (wpad - whi.astype(F32)).astype(BF16)
    tri = jnp.asarray(np.tril(np.ones((tm, tm), np.float32), -1), BF16)
    kern = functools.partial(_router_kernel, n_exp=n_exp)
    meta, cnt = pl.pallas_call(
        kern,
        grid=(T // tm,),
        in_specs=[
            pl.BlockSpec((tm, D), lambda i: (i, 0)),
            pl.BlockSpec((1, D), lambda i: (0, 0)),
            pl.BlockSpec((D, LANE), lambda i: (0, 0)),
            pl.BlockSpec((D, LANE), lambda i: (0, 0)),
            pl.BlockSpec((tm, tm), lambda i: (0, 0)),
        ],
        out_specs=[
            pl.BlockSpec((tm, LANE), lambda i: (i, 0)),
            pl.BlockSpec((8, LANE), lambda i: (0, 0)),
        ],
        out_shape=[
            jax.ShapeDtypeStruct((T, LANE), F32),
            jax.ShapeDtypeStruct((8, LANE), F32),
        ],
        scratch_shapes=[pltpu.VMEM((1, LANE), F32)],
        compiler_params=_cparams(("arbitrary",)),
        name="router",
    )(x, norm_w.reshape(1, D), whi, wlo, tri)
    return meta, cnt[0, :n_exp]


def _zero_fill_pads(padlo_ref, padlen_ref, xs_ref, z_scr, sem, *, n_exp, blk, n_blocks):
    z_scr[...] = jnp.zeros_like(z_scr)
    sub = 8
    bits = [1 << k for k in reversed(range(3, int(math.log2(blk))))]

    def pad_copies(run):
        for e in range(n_exp):
            lo = padlo_ref[e]
            ln = padlen_ref[e]
            head = (sub - lo % sub) % sub
            for r in range(sub - 1):
                @pl.when(r < head)
                def _(r=r):
                    run(pltpu.make_async_copy(z_scr.at[pl.ds(0, 1)], xs_ref.at[pl.ds(lo + r, 1)], sem))

            off = lo + head
            rem = ln - head
            for sz in bits:
                take = (rem & sz) != 0

                @pl.when(take)
                def _(off=off, sz=sz):
                    dst = xs_ref.at[pl.ds(pl.multiple_of(off, sub), sz)]
                    run(pltpu.make_async_copy(z_scr.at[pl.ds(0, sz)], dst, sem))

                off = off + jnp.where(take, sz, 0)

        def tail(b, c):
            @pl.when(b * blk >= padlo_ref[n_exp])
            def _():
                dst = xs_ref.at[pl.ds(pl.multiple_of(b * blk, blk), blk)]
                run(pltpu.make_async_copy(z_scr, dst, sem))
            return c

        lax.fori_loop(0, n_blocks, tail, 0)

    pad_copies(lambda cp: cp.start())
    pad_copies(lambda cp: cp.wait())


def _dispatch_kernel(padlo_ref, padlen_ref, d1_ref, d2_ref, x_ref, nw_ref, xs_ref, h_scr, z_scr,
                     sem, zsem, *, tm, n_exp, blk, n_blocks):
    @pl.when(pl.program_id(0) == 0)
    def _():
        _zero_fill_pads(padlo_ref, padlen_ref, xs_ref, z_scr, zsem,
                        n_exp=n_exp, blk=blk, n_blocks=n_blocks)

    i = pl.program_id(0)
    slot = i % 2
    h_scr[slot] = _rmsnorm(x_ref[...], nw_ref[...])

    def start(t, c):
        for d_ref in (d1_ref, d2_ref):
            pltpu.make_async_copy(h_scr.at[slot, pl.ds(t, 1)], xs_ref.at[pl.ds(d_ref[0, 0, t], 1)],
                                  sem.at[slot]).start()
        return c

    lax.fori_loop(0, tm, start, 0, unroll=8)

    def drain(s):
        for _ in range(TOP_K):
            pltpu.make_async_copy(h_scr.at[s], xs_ref.at[pl.ds(0, tm)], sem.at[s]).wait()

    @pl.when(i > 0)
    def _():
        drain(1 - slot)

    @pl.when(i == pl.num_programs(0) - 1)
    def _():
        drain(slot)


def _dispatch(x, norm_w, dest1, dest2, pad_lo, pad_len, n_rows, blk):
    T, D = x.shape
    tm = min(256, T)
    nb = T // tm
    n_exp = pad_len.shape[0]
    kern = functools.partial(_dispatch_kernel, tm=tm, n_exp=n_exp, blk=blk, n_blocks=n_rows // blk)
    smem_spec = pl.BlockSpec((1, 1, tm), lambda i, lo, ln: (i, 0, 0), memory_space=pltpu.SMEM)
    grid_spec = pltpu.PrefetchScalarGridSpec(
        num_scalar_prefetch=2,
        grid=(nb,),
        in_specs=[
            smem_spec,
            smem_spec,
            pl.BlockSpec((tm, D), lambda i, lo, ln: (i, 0)),
            pl.BlockSpec((1, D), lambda i, lo, ln: (0, 0)),
        ],
        out_specs=pl.BlockSpec(memory_space=pl.ANY),
        scratch_shapes=[pltpu.VMEM((2, tm, D), F32), pltpu.VMEM((blk, D), F32),
                        pltpu.SemaphoreType.DMA((2,)), pltpu.SemaphoreType.DMA(())],
    )
    return pl.pallas_call(
        kern,
        grid_spec=grid_spec,
        out_shape=jax.ShapeDtypeStruct((n_rows, D), F32),
        compiler_params=_cparams(("arbitrary",)),
        name="dispatch",
    )(pad_lo, pad_len, dest1.reshape(nb, 1, tm), dest2.reshape(nb, 1, tm), x, norm_w.reshape(1, D))


def _expert_kernel(exp_ref, nv_ref, xs_ref, wg_ref, wu_ref, wd_ref, y_ref, xb_scr):
    b = pl.program_id(0)
    nv = nv_ref[b]

    @pl.when(pl.program_id(1) == 0)
    def _():
        xb_scr[...] = xs_ref[...].astype(BF16)
        y_ref[...] = jnp.zeros_like(y_ref)

    tm = xb_scr.shape[0]
    part = tm // EXPERT_ROW_PARTS
    for rows in range(part, tm + 1, part):
        @pl.when((nv > rows - part) & (nv <= rows))
        def _(rows=rows):
            xb = xb_scr[:rows, :]
            g = _dot(xb, wg_ref[0])
            u = _dot(xb, wu_ref[0])
            a = (g * _sigmoid(g) * u).astype(BF16)
            y_ref[:rows, :] += _dot(a, wd_ref[0])


def _experts(xs, blk_exp, blk_nv, wg, wu, wd):
    n_rows, D = xs.shape
    F = wg.shape[2]
    tm = MOE_BLOCK
    tf = 512 if F % 512 == 0 else F
    grid_spec = pltpu.PrefetchScalarGridSpec(
        num_scalar_prefetch=2,
        grid=(n_rows // tm, F // tf),
        in_specs=[
            pl.BlockSpec((tm, D), lambda b, f, e, n: (b, 0)),
            pl.BlockSpec((1, D, tf), lambda b, f, e, n: (e[b], 0, f)),
            pl.BlockSpec((1, D, tf), lambda b, f, e, n: (e[b], 0, f)),
            pl.BlockSpec((1, tf, D), lambda b, f, e, n: (e[b], f, 0)),
        ],
        out_specs=pl.BlockSpec((tm, D), lambda b, f, e, n: (b, 0)),
        scratch_shapes=[pltpu.VMEM((tm, D), BF16)],
    )
    return pl.pallas_call(
        _expert_kernel,
        grid_spec=grid_spec,
        out_shape=jax.ShapeDtypeStruct((n_rows, D), F32),
        compiler_params=_cparams(("arbitrary", "arbitrary")),
        name="experts",
    )(blk_exp, blk_nv, xs, wg, wu, wd)


def _combine_kernel(d1_ref, d2_ref, n1_ref, n2_ref, x_ref, meta_ref, nw_ref, y_ref, o_ref, ybuf, sem,
                    *, tm, final_norm):
    i = pl.program_id(0)
    slot = i % 2

    def gather(r1_ref, r2_ref, s):
        def start(t, c):
            for k, r_ref in enumerate((r1_ref, r2_ref)):
                pltpu.make_async_copy(y_ref.at[pl.ds(r_ref[0, 0, t], 1)],
                                      ybuf.at[s, k, pl.ds(t, 1)], sem.at[s]).start()
            return c

        lax.fori_loop(0, tm, start, 0, unroll=8)

    @pl.when(i == 0)
    def _():
        gather(d1_ref, d2_ref, 0)

    @pl.when(i + 1 < pl.num_programs(0))
    def _():
        gather(n1_ref, n2_ref, 1 - slot)

    for k in range(TOP_K):
        pltpu.make_async_copy(y_ref.at[pl.ds(0, tm)], ybuf.at[slot, k], sem.at[slot]).wait()
    meta = meta_ref[...]
    g1 = meta[:, 2:3]
    g2 = meta[:, 3:4]
    out = x_ref[...] + (g1 * ybuf[slot, 0] + g2 * ybuf[slot, 1])
    if final_norm:
        out = _rmsnorm(out, nw_ref[...])
    o_ref[...] = out


def _combine(x, y, meta, dest1, dest2, final_w):
    T, D = x.shape
    tm = min(256, T)
    nb = T // tm
    final_norm = final_w is not None
    nw = (final_w if final_norm else jnp.ones((D,), F32)).reshape(1, D)
    kern = functools.partial(_combine_kernel, tm=tm, final_norm=final_norm)
    cur_spec = pl.BlockSpec((1, 1, tm), lambda i: (i, 0, 0), memory_space=pltpu.SMEM)
    nxt_spec = pl.BlockSpec((1, 1, tm), lambda i: (jnp.minimum(i + 1, nb - 1), 0, 0),
                            memory_space=pltpu.SMEM)
    d1 = dest1.reshape(nb, 1, tm)
    d2 = dest2.reshape(nb, 1, tm)
    return pl.pallas_call(
        kern,
        grid=(nb,),
        in_specs=[
            cur_spec,
            cur_spec,
            nxt_spec,
            nxt_spec,
            pl.BlockSpec((tm, D), lambda i: (i, 0)),
            pl.BlockSpec((tm, LANE), lambda i: (i, 0)),
            pl.BlockSpec((1, D), lambda i: (0, 0)),
            pl.BlockSpec(memory_space=pl.ANY),
        ],
        out_specs=pl.BlockSpec((tm, D), lambda i: (i, 0)),
        out_shape=jax.ShapeDtypeStruct((T, D), F32),
        scratch_shapes=[pltpu.VMEM((2, TOP_K, tm, D), F32), pltpu.SemaphoreType.DMA((2,))],
        compiler_params=_cparams(("arbitrary",)),
        name="combine",
    )(d1, d2, d1, d2, x, meta, nw, y)


def _moe(x, norm_w, w_router, wg, wu, wd, final_w):
    T, D = x.shape
    n_exp = w_router.shape[1]
    blk = MOE_BLOCK
    meta, counts_f = _router(x, norm_w, w_router)
    counts = counts_f.astype(jnp.int32)
    padded = (counts + blk - 1) // blk * blk
    pad_ends = jnp.cumsum(padded)
    pad_starts = pad_ends - padded
    n_rows = (T * TOP_K + blk - 1) // blk * blk + n_exp * blk
    n_blocks = n_rows // blk
    e1 = meta[:, 0].astype(jnp.int32)
    e2 = meta[:, 1].astype(jnp.int32)
    eids = jnp.arange(n_exp, dtype=jnp.int32)[None, :]
    start_of = lambda e: jnp.sum(jnp.where(e[:, None] == eids, pad_starts[None, :], 0), axis=1)
    dest1 = start_of(e1) + meta[:, 4].astype(jnp.int32)
    dest2 = start_of(e2) + meta[:, 5].astype(jnp.int32)
    blk_start = jnp.arange(n_blocks, dtype=jnp.int32) * blk
    blk_exp = jnp.minimum(jnp.sum((pad_ends[None, :] <= blk_start[:, None]).astype(jnp.int32), axis=1),
                          n_exp - 1)
    blk_nv = jnp.clip(pad_starts[blk_exp] + counts[blk_exp] - blk_start, 0, blk).astype(jnp.int32)
    pad_lo = jnp.concatenate([pad_starts + counts, pad_ends[-1:]]).astype(jnp.int32)
    pad_len = (padded - counts).astype(jnp.int32)
    xs = _dispatch(x, norm_w, dest1, dest2, pad_lo, pad_len, n_rows, blk)
    y = _experts(xs, blk_exp, blk_nv, wg, wu, wd)
    return _combine(x, y, meta, dest1, dest2, final_w)


def _final_norm_kernel(x_ref, nw_ref, o_ref):
    o_ref[...] = _rmsnorm(x_ref[...], nw_ref[...])


def _final_norm(x, w):
    T, D = x.shape
    tm = min(1024, T)
    return pl.pallas_call(
        _final_norm_kernel,
        grid=(T // tm,),
        in_specs=[pl.BlockSpec((tm, D), lambda i: (i, 0)), pl.BlockSpec((1, D), lambda i: (0, 0))],
        out_specs=pl.BlockSpec((tm, D), lambda i: (i, 0)),
        out_shape=jax.ShapeDtypeStruct((T, D), F32),
        compiler_params=_cparams(("arbitrary",)),
        name="final_norm",
    )(x, w.reshape(1, D))


def kernel(x, attn_norm_w, w_in, hgrn_lb_logits, hgrn_norm_w, s5_a_re, s5_a_im, s5_log_dt,
           s5_b_re, s5_b_im, s5_c_re, s5_c_im, s5_d, s5_w_glu, w_branch_a, w_branch_b,
           w_out, ffn_norm_w, dense_w_gate, dense_w_up, dense_w_down, moe_w_router,
           moe_w_gate, moe_w_up, moe_w_down, final_norm_w):
    bsz, seq, D = x.shape
    depth = w_in.shape[0]
    dh = hgrn_lb_logits.shape[1]
    T = bsz * seq
    lower = jnp.cumsum(jax.nn.softmax(hgrn_lb_logits.astype(F32), axis=0), axis=0)
    lower = lower - lower[0]
    nb = seq // S5_L
    nsteps = max(1, int(math.log2(min(S5_ROWS, nb))))
    xf = x.reshape(T, D).astype(F32)
    for layer in range(depth):
        w = w_in[layer]
        seg = lambda k, n=1: w[:, k * dh:(k + n) * dh]
        w_r = jnp.concatenate([seg(5, 2), seg(7, 2), seg(0), seg(2), seg(3), seg(1), seg(4)],
                              axis=1).astype(BF16)
        proj_b, proj_f = _norm_inproj(xf, attn_norm_w[layer], w_r, dh)
        ya = _hgrn2(proj_b, proj_f, lower[layer], hgrn_norm_w[layer], bsz, seq)
        tables = _s5_tables(s5_a_re[layer], s5_a_im[layer], s5_log_dt[layer], s5_b_re[layer],
                            s5_b_im[layer], s5_c_re[layer], s5_c_im[layer], s5_d[layer],
                            S5_L, nsteps)
        z = _s5(proj_f, 2 * dh // LANE, tables, bsz, seq)
        xf = _merge_out(xf, ya, z, proj_b, s5_w_glu[layer].astype(BF16),
                        w_branch_a[layer].astype(BF16), w_branch_b[layer].astype(BF16),
                        w_out[layer].astype(BF16))
        last = layer == depth - 1
        j = layer // 2
        if layer % 2 == 0:
            xf = _dense_ffn(xf, ffn_norm_w[layer], dense_w_gate[j].astype(BF16),
                            dense_w_up[j].astype(BF16), dense_w_down[j].astype(BF16))
            if last:
                xf = _final_norm(xf, final_norm_w)
        else:
            xf = _moe(xf, ffn_norm_w[layer], moe_w_router[j], moe_w_gate[j].astype(BF16),
                      moe_w_up[j].astype(BF16), moe_w_down[j].astype(BF16),
                      final_norm_w if last else None)
    return xf.reshape(bsz, seq, D).astype(x.dtype)
```

```python
import functools
import math

import numpy as np
import jax
import jax.numpy as jnp
from jax import lax
from jax.experimental import pallas as pl
from jax.experimental.pallas import tpu as pltpu

F32 = jnp.float32
BF16 = jnp.bfloat16

RMS_EPS = 1e-6
HGRN_HEAD_DIM = 128
S5_GROUP = 16
S5_STATE = 64
S5_MIN_NEG = 1e-4
TOP_K = 2

LANE = 128
SUBLANES = 8
VMEM_LIMIT = 58 * 1024 * 1024

HGRN_CHUNK = 128
S5_L = 16
S5_ROWS = 512
MOE_BLOCK = 1024
EXPERT_ROW_PARTS = 8
GROUPS_PER_TILE = LANE // S5_GROUP
STATE_PER_TILE = GROUPS_PER_TILE * S5_STATE


def _cparams(sem):
    return pltpu.CompilerParams(dimension_semantics=sem, vmem_limit_bytes=VMEM_LIMIT)


def _dot(a, b):
    return jnp.dot(a, b, preferred_element_type=F32)


def _dot_nt(a, b):
    return lax.dot_general(a, b, (((1,), (1,)), ((), ())), preferred_element_type=F32)


def _dot_tn(a, b):
    return lax.dot_general(a, b, (((0,), (0,)), ((), ())), preferred_element_type=F32)


def _sigmoid(x):
    return 1.0 / (1.0 + jnp.exp(-x))


def _rmsnorm(x, w):
    ms = jnp.mean(x * x, axis=-1, keepdims=True)
    return x * lax.rsqrt(ms + RMS_EPS) * w


N_BF16_COLS = 7
N_F32_COLS = 2


def _norm_inproj_kernel(x_ref, nw_ref, w_ref, ob_ref, of_ref, h_scr):
    @pl.when(pl.program_id(1) == 0)
    def _():
        h_scr[...] = _rmsnorm(x_ref[...], nw_ref[...]).astype(BF16)

    acc = _dot(h_scr[...], w_ref[...])
    ob_ref[...] = acc.astype(BF16)
    of_ref[...] = acc


def _norm_inproj(x, norm_w, w_r, dh):
    T, D = x.shape
    tm = min(1024, T)
    ncols = N_BF16_COLS + N_F32_COLS
    return pl.pallas_call(
        _norm_inproj_kernel,
        grid=(T // tm, ncols),
        in_specs=[
            pl.BlockSpec((tm, D), lambda i, j: (i, 0)),
            pl.BlockSpec((1, D), lambda i, j: (0, 0)),
            pl.BlockSpec((D, dh), lambda i, j: (0, j)),
        ],
        out_specs=[
            pl.BlockSpec((tm, dh), lambda i, j: (i, jnp.minimum(j, N_BF16_COLS))),
            pl.BlockSpec((tm, dh), lambda i, j: (i, jnp.maximum(j - N_BF16_COLS + 1, 0))),
        ],
        out_shape=[
            jax.ShapeDtypeStruct((T, (N_BF16_COLS + 1) * dh), BF16),
            jax.ShapeDtypeStruct((T, (N_F32_COLS + 1) * dh), F32),
        ],
        scratch_shapes=[pltpu.VMEM((tm, D), BF16)],
        compiler_params=_cparams(("arbitrary", "arbitrary")),
        name="norm_inproj",
    )(x, norm_w.reshape(1, D), w_r)


def _hgrn_tables(C):
    nlev = int(math.log2(C))
    r = np.arange(C)[:, None]
    c = np.arange(C)[None, :]
    tri = (c <= r).astype(np.float32)
    ltri = np.concatenate([tri, tri, tri], axis=1)
    masks = [np.eye(C, dtype=bool)]
    for lev in range(1, nlev + 1):
        b = 2 ** lev
        half = b // 2
        masks.append(((r // b) == (c // b)) & ((r % b) >= half) & ((c % b) < half))
    m = np.stack(masks, axis=0).astype(np.float32)
    rr = np.arange(C)
    rowsel = np.stack([rr % 2 == 1, rr % 4 == 0, rr % 4 >= 2, rr % 4 == 3]).astype(np.float32)
    return ltri, m, rowsel, nlev


LOG2E = 1.4426950408889634


def _hgrn2_chunk(q_pre, v, g_pre, x, lb_ref, nw_ref, ltri_ref, mask_ref, rs_ref, st_ref, p_scr,
                 *, C, nlev, heads):
    hd = HGRN_HEAD_DIM
    dh = x.shape[1]
    e = jnp.exp(-jnp.abs(x))
    log_sig = jnp.minimum(x, 0.0) - jnp.log(1.0 + e)
    a_ = lb_ref[0:1, :]
    b_ = lb_ref[1:2, :] + log_sig
    logf = jnp.maximum(a_, b_) + jnp.log(1.0 + jnp.exp(-jnp.abs(a_ - b_)))
    kk = lb_ref[2:3, :] * jnp.where(x >= 0.0, e, 1.0) / (1.0 + e)
    q = q_pre.astype(F32)
    qs_b = (q * _sigmoid(q)).astype(BF16)
    kk_b = kk.astype(BF16)

    lf2 = logf * LOG2E
    hi = lf2.astype(BF16)
    r1 = lf2 - hi.astype(F32)
    mid = r1.astype(BF16)
    lo = (r1 - mid.astype(F32)).astype(BF16)
    p = _dot(ltri_ref[...], jnp.concatenate([hi, mid, lo], axis=0))
    p_scr[...] = p

    e_q = jnp.exp2(p)
    q_in = qs_b * e_q.astype(BF16)
    dec = e_q[C - 1:C, :]
    k_st = kk_b * jnp.exp2(p_scr[C - 1:C, :] - p).astype(BF16)

    def level_decay(lev):
        if lev == 1:
            return jnp.exp2(lf2 * rs_ref[0])
        if lev == 2:
            win = (pltpu.roll(lf2, C - 1, axis=0) * rs_ref[1] + lf2 * rs_ref[2]
                   + pltpu.roll(lf2, 1, axis=0) * rs_ref[3])
            return jnp.exp2(win)
        b = 2 ** lev
        half = b // 2
        if half % 8:
            refs = [jnp.broadcast_to(p_scr[k * b + half - 1:k * b + half, :], (b, dh))
                    for k in range(C // b)]
            return jnp.exp2(-jnp.abs(p - jnp.concatenate(refs, axis=0)))
        parts = []
        for k in range(C // b):
            ref = jnp.broadcast_to(p_scr[k * b + half - 1:k * b + half, :], (half, dh))
            parts.append(ref - p_scr[k * b:k * b + half, :])
            parts.append(p_scr[k * b + half:(k + 1) * b, :] - ref)
        return jnp.exp2(jnp.concatenate(parts, axis=0))

    scores = [None] * heads
    for lev in range(nlev + 1):
        if lev == 0:
            ql = qs_b
            kl = kk_b
        else:
            e_l = level_decay(lev).astype(BF16)
            ql = qs_b * e_l
            kl = kk_b * e_l
        m = mask_ref[lev]
        for h in range(heads):
            sl = slice(h * hd, (h + 1) * hd)
            part = _dot_nt(ql[:, sl], kl[:, sl]) * m
            scores[h] = part if scores[h] is None else scores[h] + part

    outs = []
    for h in range(heads):
        sl = slice(h * hd, (h + 1) * hd)
        st = st_ref[h]
        o_h = _dot(scores[h].astype(BF16), v[:, sl]) + _dot_nt(q_in[:, sl], st.astype(BF16))
        st_ref[h] = st * dec[:, sl] + _dot_tn(v[:, sl], k_st[:, sl])
        ms = jnp.mean(o_h * o_h, axis=-1, keepdims=True)
        outs.append(o_h * lax.rsqrt(ms + RMS_EPS))
    o = jnp.concatenate(outs, axis=-1)
    g = g_pre.astype(F32)
    return (o * nw_ref[...] * (g * _sigmoid(g))).astype(BF16)


def _hgrn2_kernel(q_ref, i_ref, g_ref, f_ref, lb_ref, nw_ref, ltri_ref, mask_ref, rs_ref,
                  o_ref, st_ref, p_scr, *, C, nlev, heads):
    @pl.when(pl.program_id(0) == 0)
    def _():
        st_ref[...] = jnp.zeros_like(st_ref)

    for b in range(q_ref.shape[0]):
        o_ref[b] = _hgrn2_chunk(q_ref[b], i_ref[b], g_ref[b], f_ref[b], lb_ref, nw_ref, ltri_ref,
                                mask_ref, rs_ref, st_ref.at[b], p_scr.at[b],
                                C=C, nlev=nlev, heads=heads)


def _hgrn2(proj_b, proj_f, lb, norm_w, bsz, seq):
    T = proj_f.shape[0]
    dh = lb.shape[0]
    heads = dh // HGRN_HEAD_DIM
    C = min(HGRN_CHUNK, seq)
    ltri_np, mask_np, rowsel_np, nlev = _hgrn_tables(C)
    rowsel_np = np.ascontiguousarray(np.broadcast_to(rowsel_np[:, :, None], rowsel_np.shape + (dh,)))
    nc = seq // C
    lbp = jnp.stack([jnp.log(lb), jnp.log1p(-lb), 1.0 - lb], axis=0)
    lbp = jnp.concatenate([lbp, jnp.zeros((5, dh), F32)], axis=0)
    nw = jnp.tile(norm_w.astype(F32), heads).reshape(1, dh)
    pb = proj_b.reshape(bsz, seq, proj_b.shape[1])
    pf = proj_f.reshape(bsz, seq, proj_f.shape[1])
    tile = lambda k: pl.BlockSpec((bsz, C, dh), lambda c: (0, c, k))
    kern = functools.partial(_hgrn2_kernel, C=C, nlev=nlev, heads=heads)
    out = pl.pallas_call(
        kern,
        grid=(nc,),
        in_specs=[
            tile(4), tile(5), tile(6), tile(1),
            pl.BlockSpec((8, dh), lambda c: (0, 0)),
            pl.BlockSpec((1, dh), lambda c: (0, 0)),
            pl.BlockSpec(ltri_np.shape, lambda c: (0, 0)),
            pl.BlockSpec(mask_np.shape, lambda c: (0, 0, 0)),
            pl.BlockSpec(rowsel_np.shape, lambda c: (0, 0, 0)),
        ],
        out_specs=tile(0),
        out_shape=jax.ShapeDtypeStruct((bsz, seq, dh), BF16),
        scratch_shapes=[pltpu.VMEM((bsz, heads, HGRN_HEAD_DIM, HGRN_HEAD_DIM), F32),
                        pltpu.VMEM((bsz, C, dh), F32)],
        compiler_params=_cparams(("arbitrary",)),
        name="hgrn2",
    )(pb, pb, pb, pf, lbp, nw, jnp.asarray(ltri_np, BF16), jnp.asarray(mask_np, F32),
      jnp.asarray(rowsel_np, F32))
    return out.reshape(T, dh)


def _s5_tables(a_re, a_im, log_dt, b_re, b_im, c_re, c_im, d_skip, L, nsteps):
    G, P = a_re.shape
    I = S5_GROUP
    gpt = GROUPS_PER_TILE
    nq = G // gpt
    sp = STATE_PER_TILE
    hp = lax.Precision.HIGHEST
    A = lax.complex(jnp.minimum(a_re.astype(F32), -S5_MIN_NEG), a_im.astype(F32))
    dt = jnp.exp(log_dt.astype(F32))[:, None]
    adt = A * dt
    a_bar = jnp.exp(adt)
    B = lax.complex(b_re.astype(F32), b_im.astype(F32))
    b_bar = ((a_bar - 1.0) / A)[..., None] * B
    Cc = lax.complex(c_re.astype(F32), c_im.astype(F32))
    tau = jnp.arange(L + 1, dtype=F32)
    apow = jnp.exp(adt[:, None, :] * tau[None, :, None])

    def ri_lanes(z):
        lead = z.shape[:-2]
        return jnp.concatenate([z.real.reshape(*lead, nq, sp), z.imag.reshape(*lead, nq, sp)], axis=-1)

    kt = jnp.einsum('gip,gtp,gpj->gtij', Cc, apow[:, :L], b_bar, precision=hp).real
    def block_diag(a):
        cdim = a.shape[-1]
        own = (jnp.arange(gpt * cdim) // cdim)[None, :] == jnp.arange(gpt)[:, None]
        tiled = jnp.tile(a, (1, 1, 1, gpt))
        out = jnp.where(own[None, :, None, :], tiled, 0.0)
        return out.reshape(a.shape[0], gpt * a.shape[2], gpt * cdim)

    kt = kt.reshape(nq, gpt, L, I, I).transpose(0, 2, 1, 4, 3)
    kbd = block_diag(kt.reshape(nq * L, gpt, I, I)).reshape(nq, L, LANE, LANE)
    kbd = kbd.transpose(0, 2, 1, 3).reshape(nq, LANE, L * LANE)
    kstrip = jnp.concatenate([jnp.zeros((nq, LANE, (L - 1) * LANE), F32), kbd], axis=-1).astype(BF16)

    bb = b_bar.reshape(nq, gpt, P, I).transpose(0, 1, 3, 2)
    bbase = jnp.concatenate([block_diag(bb.real), block_diag(bb.imag)], axis=-1)
    cc = Cc.reshape(nq, gpt, I, P).transpose(0, 1, 3, 2)
    cbase = jnp.concatenate([block_diag(cc.real), block_diag(cc.imag)], axis=1)

    aprow = ri_lanes(apow[:, L - 1 - jnp.arange(L)].transpose(1, 0, 2)).transpose(1, 0, 2)
    apc = ri_lanes(apow[:, 1:].transpose(1, 0, 2)).transpose(1, 2, 0)
    apcol = jnp.concatenate([apc, jnp.zeros((nq, 2 * sp, LANE - L), F32)], axis=-1)

    steps = (L * (2.0 ** jnp.arange(nsteps, dtype=F32)))
    alp = ri_lanes(jnp.exp(adt[None] * steps[:, None, None])).transpose(1, 0, 2)
    within = L * (1.0 + jnp.arange(SUBLANES, dtype=F32))
    apsub = ri_lanes(jnp.exp(adt[None] * within[:, None, None])).transpose(1, 0, 2)
    apsub = jnp.tile(apsub, (1, (2 ** nsteps) // SUBLANES, 1))

    dflat = jnp.tile(d_skip.astype(F32).reshape(nq, 1, LANE), (1, L, 1)).reshape(nq, 1, L * LANE)
    return kstrip, bbase, cbase, aprow, apcol, alp, apsub, dflat


def _s5_expand(k_ref, bb_ref, cb_ref, ar_ref, ac_ref, t_scr, b_scr, c_scr, *, L):
    sp = STATE_PER_TILE
    br = bb_ref[0, :, :sp]
    bi = bb_ref[0, :, sp:]
    cr = cb_ref[0, :sp, :]
    ci = cb_ref[0, sp:, :]
    for t in range(L):
        rows = slice(t * LANE, (t + 1) * LANE)
        off = (L - 1 - t) * LANE
        t_scr[rows, :] = k_ref[0, :, off:off + L * LANE]
        ar = ar_ref[0, t:t + 1, :sp]
        ai = ar_ref[0, t:t + 1, sp:]
        b_scr[rows, :sp] = (br * ar - bi * ai).astype(BF16)
        b_scr[rows, sp:] = (br * ai + bi * ar).astype(BF16)
        acr = ac_ref[0, :sp, t:t + 1]
        aci = ac_ref[0, sp:, t:t + 1]
        c_scr[:sp, rows] = (cr * acr - ci * aci).astype(BF16)
        c_scr[sp:, rows] = (-(cr * aci + ci * acr)).astype(BF16)


def _s5_kernel(u_ref, k_ref, bb_ref, cb_ref, ar_ref, ac_ref, al_ref, ap_ref, d_ref, z_ref,
               carry_ref, t_scr, b_scr, c_scr, loc_scr, ent_scr, *, R, L, nsteps):
    sp = STATE_PER_TILE

    @pl.when((pl.program_id(1) == 0) & (pl.program_id(2) == 0))
    def _():
        _s5_expand(k_ref, bb_ref, cb_ref, ar_ref, ac_ref, t_scr, b_scr, c_scr, L=L)

    @pl.when(pl.program_id(2) == 0)
    def _():
        carry_ref[...] = jnp.zeros_like(carry_ref)

    u32 = jnp.concatenate([u_ref[pl.ds(t, R, stride=L), :] for t in range(L)], axis=-1)
    u = u32.astype(BF16)
    z = _dot(u, b_scr[...])
    re = z[:, :sp]
    im = z[:, sp:]
    cre = carry_ref[:, :sp]
    cim = carry_ref[:, sp:]
    G = R // SUBLANES

    def scan_steps(re, im, pos, k0, nk):
        for i in range(nk):
            d = 1 << i
            p_re = al_ref[0, k0 + i:k0 + i + 1, :sp]
            p_im = al_ref[0, k0 + i:k0 + i + 1, sp:]
            keep = pos >= d
            s_re = jnp.where(keep, pltpu.roll(re, d, axis=0), 0.0)
            s_im = jnp.where(keep, pltpu.roll(im, d, axis=0), 0.0)
            re, im = re + p_re * s_re - p_im * s_im, im + p_re * s_im + p_im * s_re
        return re, im

    row = lax.broadcasted_iota(jnp.int32, (R, sp), 0)
    nk1 = int(math.log2(SUBLANES))
    re, im = scan_steps(re, im, row & (SUBLANES - 1), 0, nk1)
    ntile = 2 * sp // LANE
    for c in range(ntile):
        src = re if c < ntile // 2 else im
        c0 = (c % (ntile // 2)) * LANE
        loc_scr[c] = src[:, c0:c0 + LANE]
    ends = jnp.concatenate([loc_scr[c, pl.ds(SUBLANES - 1, G, stride=SUBLANES), :]
                            for c in range(ntile)], axis=-1)
    grow = lax.broadcasted_iota(jnp.int32, (G, sp), 0)
    gfirst = grow == 0
    a_re = al_ref[0, nk1:nk1 + 1, :sp]
    a_im = al_ref[0, nk1:nk1 + 1, sp:]
    e_re = ends[:, :sp] + jnp.where(gfirst, a_re * cre - a_im * cim, 0.0)
    e_im = ends[:, sp:] + jnp.where(gfirst, a_re * cim + a_im * cre, 0.0)
    e_re, e_im = scan_steps(e_re, e_im, grow, nk1, nsteps - nk1)
    ent = jnp.concatenate([jnp.where(gfirst, cre, pltpu.roll(e_re, 1, axis=0)),
                           jnp.where(gfirst, cim, pltpu.roll(e_im, 1, axis=0))], axis=-1)
    for c in range(ntile):
        for j in range(SUBLANES):
            ent_scr[c, pl.ds(j, G, stride=SUBLANES), :] = ent[:, c * LANE:(c + 1) * LANE]
    b_re = jnp.concatenate([ent_scr[c] for c in range(ntile // 2)], axis=-1)
    b_im = jnp.concatenate([ent_scr[c] for c in range(ntile // 2, ntile)], axis=-1)
    p_re = ap_ref[0, :, :sp]
    p_im = ap_ref[0, :, sp:]
    re, im = re + p_re * b_re - p_im * b_im, im + p_re * b_im + p_im * b_re
    carry_ref[:, :sp] = e_re[G - 1:G, :]
    carry_ref[:, sp:] = e_im[G - 1:G, :]
    first = row == 0
    h_re = jnp.where(first, cre, pltpu.roll(re, 1, axis=0))
    h_im = jnp.where(first, cim, pltpu.roll(im, 1, axis=0))
    hprev = jnp.concatenate([h_re, h_im], axis=-1).astype(BF16)
    wide = 2 * LANE
    intra = jnp.concatenate(
        [_dot(u[:, :c0 + wide], t_scr[:c0 + wide, c0:c0 + wide]) for c0 in range(0, L * LANE, wide)],
        axis=-1)
    y = intra + _dot(hprev, c_scr[...]) + d_ref[0] * u32
    zz = 0.5 * y * (1.0 + jnp.tanh(math.sqrt(2.0 / math.pi) * (y + 0.044715 * (y * y * y))))
    for t in range(L):
        z_ref[pl.ds(t, R, stride=L), :] = zz[:, t * LANE:(t + 1) * LANE]


def _s5(proj_f, u_tile0, tables, bsz, seq):
    kstrip, bbase, cbase, aprow, apcol, alp, apsub, dflat = tables
    nq = kstrip.shape[0]
    T = proj_f.shape[0]
    L = aprow.shape[1]
    sp2 = 2 * STATE_PER_TILE
    nb = seq // L
    R = min(S5_ROWS, nb)
    assert R % SUBLANES == 0 and R == apsub.shape[1]
    nrb = nb // R
    nsteps = alp.shape[1]
    kern = functools.partial(_s5_kernel, R=R, L=L, nsteps=nsteps)
    per_q = lambda shape: pl.BlockSpec((1,) + shape, lambda q, b, r: (q, 0, 0))
    return pl.pallas_call(
        kern,
        grid=(nq, bsz, nrb),
        in_specs=[
            pl.BlockSpec((R * L, LANE), lambda q, b, r: (b * nrb + r, u_tile0 + q)),
            per_q((LANE, (2 * L - 1) * LANE)),
            per_q((LANE, sp2)),
            per_q((sp2, LANE)),
            per_q((L, sp2)),
            per_q((sp2, LANE)),
            per_q((nsteps, sp2)),
            per_q((R, sp2)),
            per_q((1, L * LANE)),
        ],
        out_specs=pl.BlockSpec((R * L, LANE), lambda q, b, r: (b * nrb + r, q)),
        out_shape=jax.ShapeDtypeStruct((T, nq * LANE), F32),
        scratch_shapes=[
            pltpu.VMEM((1, sp2), F32),
            pltpu.VMEM((L * LANE, L * LANE), BF16),
            pltpu.VMEM((L * LANE, sp2), BF16),
            pltpu.VMEM((sp2, L * LANE), BF16),
            pltpu.VMEM((sp2 // LANE, R, LANE), F32),
            pltpu.VMEM((sp2 // LANE, R, LANE), F32),
        ],
        compiler_params=_cparams(("arbitrary", "arbitrary", "arbitrary")),
        name="s5",
    )(proj_f, kstrip, bbase, cbase, aprow, apcol, alp, apsub, dflat)


def _merge_out_kernel(x_ref, ya_ref, z_ref, ga_ref, gb_ref, wglu_ref, wa_ref, wb_ref, wo_ref, o_ref):
    z = z_ref[...]
    yb = (z * _sigmoid(_dot(z.astype(BF16), wglu_ref[...]))).astype(BF16)
    ga = _sigmoid(ga_ref[...].astype(F32))
    gb = _sigmoid(gb_ref[...].astype(F32))
    m = ga * _dot(ya_ref[...], wa_ref[...]) + gb * _dot(yb, wb_ref[...])
    o_ref[...] = x_ref[...] + _dot(m.astype(BF16), wo_ref[...])


def _merge_out(x, ya, z, proj_b, wglu, wa, wb, wout):
    T, D = x.shape
    dh = ya.shape[1]
    ds5 = z.shape[1]
    tm = min(512, T)
    full = lambda a: pl.BlockSpec(a.shape, lambda i: (0, 0))
    return pl.pallas_call(
        _merge_out_kernel,
        grid=(T // tm,),
        in_specs=[
            pl.BlockSpec((tm, D), lambda i: (i, 0)),
            pl.BlockSpec((tm, dh), lambda i: (i, 0)),
            pl.BlockSpec((tm, ds5), lambda i: (i, 0)),
            pl.BlockSpec((tm, D), lambda i: (i, 0)),
            pl.BlockSpec((tm, D), lambda i: (i, 1)),
            full(wglu), full(wa), full(wb), full(wout),
        ],
        out_specs=pl.BlockSpec((tm, D), lambda i: (i, 0)),
        out_shape=jax.ShapeDtypeStruct((T, D), F32),
        compiler_params=_cparams(("arbitrary",)),
        name="merge_out",
    )(x, ya, z, proj_b, proj_b, wglu, wa, wb, wout)


def _dense_ffn_kernel(x_ref, nw_ref, wg_ref, wu_ref, wd_ref, o_ref, h_scr):
    @pl.when(pl.program_id(1) == 0)
    def _():
        x = x_ref[...]
        h_scr[...] = _rmsnorm(x, nw_ref[...]).astype(BF16)
        o_ref[...] = x

    h = h_scr[...]
    g = _dot(h, wg_ref[...])
    u = _dot(h, wu_ref[...])
    a = (g * _sigmoid(g) * u).astype(BF16)
    o_ref[...] += _dot(a, wd_ref[...])


def _dense_ffn(x, norm_w, wg, wu, wd):
    T, D = x.shape
    F = wg.shape[1]
    tm = min(1024, T)
    tf = 512 if F % 512 == 0 else F
    return pl.pallas_call(
        _dense_ffn_kernel,
        grid=(T // tm, F // tf),
        in_specs=[
            pl.BlockSpec((tm, D), lambda i, f: (i, 0)),
            pl.BlockSpec((1, D), lambda i, f: (0, 0)),
            pl.BlockSpec((D, tf), lambda i, f: (0, f)),
            pl.BlockSpec((D, tf), lambda i, f: (0, f)),
            pl.BlockSpec((tf, D), lambda i, f: (f, 0)),
        ],
        out_specs=pl.BlockSpec((tm, D), lambda i, f: (i, 0)),
        out_shape=jax.ShapeDtypeStruct((T, D), F32),
        scratch_shapes=[pltpu.VMEM((tm, D), BF16)],
        compiler_params=_cparams(("arbitrary", "arbitrary")),
        name="dense_ffn",
    )(x, norm_w.reshape(1, D), wg, wu, wd)


def _router_kernel(x_ref, nw_ref, whi_ref, wlo_ref, tri_ref, meta_ref, cnt_ref, run_ref, *, n_exp):
    i = pl.program_id(0)

    @pl.when(i == 0)
    def _():
        run_ref[...] = jnp.zeros_like(run_ref)

    h = _rmsnorm(x_ref[...], nw_ref[...])
    hi = h.astype(BF16)
    lo = (h - hi.astype(F32)).astype(BF16)
    logits = _dot(hi, whi_ref[...]) + _dot(lo, whi_ref[...]) + _dot(hi, wlo_ref[...])
    tm = logits.shape[0]
    lane = lax.broadcasted_iota(jnp.int32, (tm, LANE), 1)
    neg = jnp.float32(-jnp.inf)
    l1 = jnp.where(lane < n_exp, logits, neg)
    m1 = jnp.max(l1, axis=-1, keepdims=True)
    i1 = jnp.min(jnp.where(l1 == m1, lane, LANE), axis=-1, keepdims=True)
    l2 = jnp.where(lane == i1, neg, l1)
    m2 = jnp.max(l2, axis=-1, keepdims=True)
    i2 = jnp.min(jnp.where(l2 == m2, lane, LANE), axis=-1, keepdims=True)
    g1 = 1.0 / (1.0 + jnp.exp(m2 - m1))
    g2 = 1.0 - g1
    sel1 = lane == i1
    sel2 = lane == i2
    twohot = (sel1 | sel2).astype(F32)
    before = _dot(tri_ref[...], twohot.astype(BF16)) + run_ref[...]
    p1 = jnp.sum(jnp.where(sel1, before, 0.0), axis=-1, keepdims=True)
    p2 = jnp.sum(jnp.where(sel2, before, 0.0), axis=-1, keepdims=True)
    run = run_ref[...] + jnp.sum(twohot, axis=0, keepdims=True)
    run_ref[...] = run
    cnt_ref[...] = jnp.broadcast_to(run, cnt_ref.shape)
    meta = jnp.where(lane == 0, i1.astype(F32), 0.0)
    meta = jnp.where(lane == 1, i2.astype(F32), meta)
    meta = jnp.where(lane == 2, g1, meta)
    meta = jnp.where(lane == 3, g2, meta)
    meta = jnp.where(lane == 4, p1, meta)
    meta = jnp.where(lane == 5, p2, meta)
    meta_ref[...] = meta


def _router(x, norm_w, w_router):
    T, D = x.shape
    n_exp = w_router.shape[1]
    tm = min(512, T)
    wpad = jnp.zeros((D, LANE), F32).at[:, :n_exp].set(w_router.astype(F32))
    whi = wpad.astype(BF16)
    wlo = (wpad - whi.astype(F32)).astype(BF16)
    tri = jnp.asarray(np.tril(np.ones((tm, tm), np.float32), -1), BF16)
    kern = functools.partial(_router_kernel, n_exp=n_exp)
    meta, cnt = pl.pallas_call(
        kern,
        grid=(T // tm,),
        in_specs=[
            pl.BlockSpec((tm, D), lambda i: (i, 0)),
            pl.BlockSpec((1, D), lambda i: (0, 0)),
            pl.BlockSpec((D, LANE), lambda i: (0, 0)),
            pl.BlockSpec((D, LANE), lambda i: (0, 0)),
            pl.BlockSpec((tm, tm), lambda i: (0, 0)),
        ],
        out_specs=[
            pl.BlockSpec((tm, LANE), lambda i: (i, 0)),
            pl.BlockSpec((8, LANE), lambda i: (0, 0)),
        ],
        out_shape=[
            jax.ShapeDtypeStruct((T, LANE), F32),
            jax.ShapeDtypeStruct((8, LANE), F32),
        ],
        scratch_shapes=[pltpu.VMEM((1, LANE), F32)],
        compiler_params=_cparams(("arbitrary",)),
        name="router",
    )(x, norm_w.reshape(1, D), whi, wlo, tri)
    return meta, cnt[0, :n_exp]


def _zero_fill_pads(padlo_ref, padlen_ref, xs_ref, z_scr, sem, *, n_exp, blk, n_blocks):
    z_scr[...] = jnp.zeros_like(z_scr)
    sub = 8
    bits = [1 << k for k in reversed(range(3, int(math.log2(blk))))]

    def pad_copies(run):
        for e in range(n_exp):
            lo = padlo_ref[e]
            ln = padlen_ref[e]
            head = (sub - lo % sub) % sub
            for r in range(sub - 1):
                @pl.when(r < head)
                def _(r=r):
                    run(pltpu.make_async_copy(z_scr.at[pl.ds(0, 1)], xs_ref.at[pl.ds(lo + r, 1)], sem))

            off = lo + head
            rem = ln - head
            for sz in bits:
                take = (rem & sz) != 0

                @pl.when(take)
                def _(off=off, sz=sz):
                    dst = xs_ref.at[pl.ds(pl.multiple_of(off, sub), sz)]
                    run(pltpu.make_async_copy(z_scr.at[pl.ds(0, sz)], dst, sem))

                off = off + jnp.where(take, sz, 0)

        def tail(b, c):
            @pl.when(b * blk >= padlo_ref[n_exp])
            def _():
                dst = xs_ref.at[pl.ds(pl.multiple_of(b * blk, blk), blk)]
                run(pltpu.make_async_copy(z_scr, dst, sem))
            return c

        lax.fori_loop(0, n_blocks, tail, 0)

    pad_copies(lambda cp: cp.start())
    pad_copies(lambda cp: cp.wait())


def _dispatch_kernel(padlo_ref, padlen_ref, d1_ref, d2_ref, x_ref, nw_ref, xs_ref, h_scr, z_scr,
                     sem, zsem, *, tm, n_exp, blk, n_blocks):
    @pl.when(pl.program_id(0) == 0)
    def _():
        _zero_fill_pads(padlo_ref, padlen_ref, xs_ref, z_scr, zsem,
                        n_exp=n_exp, blk=blk, n_blocks=n_blocks)

    i = pl.program_id(0)
    slot = i % 2
    h_scr[slot] = _rmsnorm(x_ref[...], nw_ref[...])

    def start(t, c):
        for d_ref in (d1_ref, d2_ref):
            pltpu.make_async_copy(h_scr.at[slot, pl.ds(t, 1)], xs_ref.at[pl.ds(d_ref[0, 0, t], 1)],
                                  sem.at[slot]).start()
        return c

    lax.fori_loop(0, tm, start, 0, unroll=8)

    def drain(s):
        for _ in range(TOP_K):
            pltpu.make_async_copy(h_scr.at[s], xs_ref.at[pl.ds(0, tm)], sem.at[s]).wait()

    @pl.when(i > 0)
    def _():
        drain(1 - slot)

    @pl.when(i == pl.num_programs(0) - 1)
    def _():
        drain(slot)


def _dispatch(x, norm_w, dest1, dest2, pad_lo, pad_len, n_rows, blk):
    T, D = x.shape
    tm = min(512, T)
    nb = T // tm
    n_exp = pad_len.shape[0]
    kern = functools.partial(_dispatch_kernel, tm=tm, n_exp=n_exp, blk=blk, n_blocks=n_rows // blk)
    smem_spec = pl.BlockSpec((1, 1, tm), lambda i, lo, ln: (i, 0, 0), memory_space=pltpu.SMEM)
    grid_spec = pltpu.PrefetchScalarGridSpec(
        num_scalar_prefetch=2,
        grid=(nb,),
        in_specs=[
            smem_spec,
            smem_spec,
            pl.BlockSpec((tm, D), lambda i, lo, ln: (i, 0)),
            pl.BlockSpec((1, D), lambda i, lo, ln: (0, 0)),
        ],
        out_specs=pl.BlockSpec(memory_space=pl.ANY),
        scratch_shapes=[pltpu.VMEM((2, tm, D), F32), pltpu.VMEM((blk, D), F32),
                        pltpu.SemaphoreType.DMA((2,)), pltpu.SemaphoreType.DMA(())],
    )
    return pl.pallas_call(
        kern,
        grid_spec=grid_spec,
        out_shape=jax.ShapeDtypeStruct((n_rows, D), F32),
        compiler_params=_cparams(("arbitrary",)),
        name="dispatch",
    )(pad_lo, pad_len, dest1.reshape(nb, 1, tm), dest2.reshape(nb, 1, tm), x, norm_w.reshape(1, D))


def _expert_kernel(exp_ref, nv_ref, xs_ref, wg_ref, wu_ref, wd_ref, y_ref, xb_scr):
    b = pl.program_id(0)
    nv = nv_ref[b]

    @pl.when(pl.program_id(1) == 0)
    def _():
        xb_scr[...] = xs_ref[...].astype(BF16)
        y_ref[...] = jnp.zeros_like(y_ref)

    tm = xb_scr.shape[0]
    part = tm // EXPERT_ROW_PARTS
    for rows in range(part, tm + 1, part):
        @pl.when((nv > rows - part) & (nv <= rows))
        def _(rows=rows):
            xb = xb_scr[:rows, :]
            g = _dot(xb, wg_ref[0])
            u = _dot(xb, wu_ref[0])
            a = (g * _sigmoid(g) * u).astype(BF16)
            y_ref[:rows, :] += _dot(a, wd_ref[0])


def _experts(xs, blk_exp, blk_nv, wg, wu, wd):
    n_rows, D = xs.shape
    F = wg.shape[2]
    tm = MOE_BLOCK
    tf = 512 if F % 512 == 0 else F
    grid_spec = pltpu.PrefetchScalarGridSpec(
        num_scalar_prefetch=2,
        grid=(n_rows // tm, F // tf),
        in_specs=[
            pl.BlockSpec((tm, D), lambda b, f, e, n: (b, 0)),
            pl.BlockSpec((1, D, tf), lambda b, f, e, n: (e[b], 0, f)),
            pl.BlockSpec((1, D, tf), lambda b, f, e, n: (e[b], 0, f)),
            pl.BlockSpec((1, tf, D), lambda b, f, e, n: (e[b], f, 0)),
        ],
        out_specs=pl.BlockSpec((tm, D), lambda b, f, e, n: (b, 0)),
        scratch_shapes=[pltpu.VMEM((tm, D), BF16)],
    )
    return pl.pallas_call(
        _expert_kernel,
        grid_spec=grid_spec,
        out_shape=jax.ShapeDtypeStruct((n_rows, D), F32),
        compiler_params=_cparams(("arbitrary", "arbitrary")),
        name="experts",
    )(blk_exp, blk_nv, xs, wg, wu, wd)


def _combine_kernel(d1_ref, d2_ref, n1_ref, n2_ref, x_ref, meta_ref, nw_ref, y_ref, o_ref, ybuf, sem,
                    *, tm, final_norm):
    i = pl.program_id(0)
    slot = i % 2

    def gather(r1_ref, r2_ref, s):
        def start(t, c):
            for k, r_ref in enumerate((r1_ref, r2_ref)):
                pltpu.make_async_copy(y_ref.at[pl.ds(r_ref[0, 0, t], 1)],
                                      ybuf.at[s, k, pl.ds(t, 1)], sem.at[s]).start()
            return c

        lax.fori_loop(0, tm, start, 0, unroll=8)

    @pl.when(i == 0)
    def _():
        gather(d1_ref, d2_ref, 0)

    @pl.when(i + 1 < pl.num_programs(0))
    def _():
        gather(n1_ref, n2_ref, 1 - slot)

    for k in range(TOP_K):
        pltpu.make_async_copy(y_ref.at[pl.ds(0, tm)], ybuf.at[slot, k], sem.at[slot]).wait()
    meta = meta_ref[...]
    g1 = meta[:, 2:3]
    g2 = meta[:, 3:4]
    out = x_ref[...] + (g1 * ybuf[slot, 0] + g2 * ybuf[slot, 1])
    if final_norm:
        out = _rmsnorm(out, nw_ref[...])
    o_ref[...] = out


def _combine(x, y, meta, dest1, dest2, final_w):
    T, D = x.shape
    tm = min(512, T)
    nb = T // tm
    final_norm = final_w is not None
    nw = (final_w if final_norm else jnp.ones((D,), F32)).reshape(1, D)
    kern = functools.partial(_combine_kernel, tm=tm, final_norm=final_norm)
    cur_spec = pl.BlockSpec((1, 1, tm), lambda i: (i, 0, 0), memory_space=pltpu.SMEM)
    nxt_spec = pl.BlockSpec((1, 1, tm), lambda i: (jnp.minimum(i + 1, nb - 1), 0, 0),
                            memory_space=pltpu.SMEM)
    d1 = dest1.reshape(nb, 1, tm)
    d2 = dest2.reshape(nb, 1, tm)
    return pl.pallas_call(
        kern,
        grid=(nb,),
        in_specs=[
            cur_spec,
            cur_spec,
            nxt_spec,
            nxt_spec,
            pl.BlockSpec((tm, D), lambda i: (i, 0)),
            pl.BlockSpec((tm, LANE), lambda i: (i, 0)),
            pl.BlockSpec((1, D), lambda i: (0, 0)),
            pl.BlockSpec(memory_space=pl.ANY),
        ],
        out_specs=pl.BlockSpec((tm, D), lambda i: (i, 0)),
        out_shape=jax.ShapeDtypeStruct((T, D), F32),
        scratch_shapes=[pltpu.VMEM((2, TOP_K, tm, D), F32), pltpu.SemaphoreType.DMA((2,))],
        compiler_params=_cparams(("arbitrary",)),
        name="combine",
    )(d1, d2, d1, d2, x, meta, nw, y)


def _moe(x, norm_w, w_router, wg, wu, wd, final_w):
    T, D = x.shape
    n_exp = w_router.shape[1]
    blk = MOE_BLOCK
    meta, counts_f = _router(x, norm_w, w_router)
    counts = counts_f.astype(jnp.int32)
    padded = (counts + blk - 1) // blk * blk
    pad_ends = jnp.cumsum(padded)
    pad_starts = pad_ends - padded
    n_rows = (T * TOP_K + blk - 1) // blk * blk + n_exp * blk
    n_blocks = n_rows // blk
    e1 = meta[:, 0].astype(jnp.int32)
    e2 = meta[:, 1].astype(jnp.int32)
    eids = jnp.arange(n_exp, dtype=jnp.int32)[None, :]
    start_of = lambda e: jnp.sum(jnp.where(e[:, None] == eids, pad_starts[None, :], 0), axis=1)
    dest1 = start_of(e1) + meta[:, 4].astype(jnp.int32)
    dest2 = start_of(e2) + meta[:, 5].astype(jnp.int32)
    blk_start = jnp.arange(n_blocks, dtype=jnp.int32) * blk
    blk_exp = jnp.minimum(jnp.sum((pad_ends[None, :] <= blk_start[:, None]).astype(jnp.int32), axis=1),
                          n_exp - 1)
    blk_nv = jnp.clip(pad_starts[blk_exp] + counts[blk_exp] - blk_start, 0, blk).astype(jnp.int32)
    pad_lo = jnp.concatenate([pad_starts + counts, pad_ends[-1:]]).astype(jnp.int32)
    pad_len = (padded - counts).astype(jnp.int32)
    xs = _dispatch(x, norm_w, dest1, dest2, pad_lo, pad_len, n_rows, blk)
    y = _experts(xs, blk_exp, blk_nv, wg, wu, wd)
    return _combine(x, y, meta, dest1, dest2, final_w)


def _final_norm_kernel(x_ref, nw_ref, o_ref):
    o_ref[...] = _rmsnorm(x_ref[...], nw_ref[...])


def _final_norm(x, w):
    T, D = x.shape
    tm = min(1024, T)
    return pl.pallas_call(
        _final_norm_kernel,
        grid=(T // tm,),
        in_specs=[pl.BlockSpec((tm, D), lambda i: (i, 0)), pl.BlockSpec((1, D), lambda i: (0, 0))],
        out_specs=pl.BlockSpec((tm, D), lambda i: (i, 0)),
        out_shape=jax.ShapeDtypeStruct((T, D), F32),
        compiler_params=_cparams(("arbitrary",)),
        name="final_norm",
    )(x, w.reshape(1, D))


def kernel(x, attn_norm_w, w_in, hgrn_lb_logits, hgrn_norm_w, s5_a_re, s5_a_im, s5_log_dt,
           s5_b_re, s5_b_im, s5_c_re, s5_c_im, s5_d, s5_w_glu, w_branch_a, w_branch_b,
           w_out, ffn_norm_w, dense_w_gate, dense_w_up, dense_w_down, moe_w_router,
           moe_w_gate, moe_w_up, moe_w_down, final_norm_w):
    bsz, seq, D = x.shape
    depth = w_in.shape[0]
    dh = hgrn_lb_logits.shape[1]
    T = bsz * seq
    lower = jnp.cumsum(jax.nn.softmax(hgrn_lb_logits.astype(F32), axis=0), axis=0)
    lower = lower - lower[0]
    nb = seq // S5_L
    nsteps = max(1, int(math.log2(min(S5_ROWS, nb))))
    xf = x.reshape(T, D).astype(F32)
    for layer in range(depth):
        w = w_in[layer]
        seg = lambda k, n=1: w[:, k * dh:(k + n) * dh]
        w_r = jnp.concatenate([seg(5, 2), seg(7, 2), seg(0), seg(2), seg(3), seg(1), seg(4)],
                              axis=1).astype(BF16)
        proj_b, proj_f = _norm_inproj(xf, attn_norm_w[layer], w_r, dh)
        ya = _hgrn2(proj_b, proj_f, lower[layer], hgrn_norm_w[layer], bsz, seq)
        tables = _s5_tables(s5_a_re[layer], s5_a_im[layer], s5_log_dt[layer], s5_b_re[layer],
                            s5_b_im[layer], s5_c_re[layer], s5_c_im[layer], s5_d[layer],
                            S5_L, nsteps)
        z = _s5(proj_f, 2 * dh // LANE, tables, bsz, seq)
        xf = _merge_out(xf, ya, z, proj_b, s5_w_glu[layer].astype(BF16),
                        w_branch_a[layer].astype(BF16), w_branch_b[layer].astype(BF16),
                        w_out[layer].astype(BF16))
        last = layer == depth - 1
        j = layer // 2
        if layer % 2 == 0:
            xf = _dense_ffn(xf, ffn_norm_w[layer], dense_w_gate[j].astype(BF16),
                            dense_w_up[j].astype(BF16), dense_w_down[j].astype(BF16))
            if last:
                xf = _final_norm(xf, final_norm_w)
        else:
            xf = _moe(xf, ffn_norm_w[layer], moe_w_router[j], moe_w_gate[j].astype(BF16),
                      moe_w_up[j].astype(BF16), moe_w_down[j].astype(BF16),
                      final_norm_w if last else None)
    return xf.reshape(bsz, seq, D).astype(x.dtype)
```
